```python
import math
import jax, jax.numpy as jnp
from jax import lax
import numpy as np

D_MODEL = 1024
BATCH = 2
SEQ = 16384
DEPTH = 2
DEC_BATCH = 1
DEC_SEQ = 16384
PAST_LEN = 128

N_EVEN = (DEPTH + 1) // 2
N_ODD = DEPTH // 2
N_SUB = 3
D_FF = 2816
EPS = 1e-6
D_A = D_MODEL // 2
CONV_W = 31
D_B = D_MODEL // 2
HY_ORDER = 2
HY_SHORT = 3
HY_EMB = 33
HY_BANDS = (HY_EMB - 1) // 2
HY_HIDDEN = 64
HY_FAST_PCT = 0.3
HY_SLOW_PCT = 1.5
HY_TARGET = 1e-2
D_AB_IN = 2 * D_A + (HY_ORDER + 1) * D_B
N_HEADS = 16
N_KV = 4
GROUP = N_HEADS // N_KV
HEAD_DIM = 64
WINDOW = 128
BLOCK = 128
NEG = -1e30

kernel_name = "hybrid_conv_hyena_swa_encoder"


def rms_norm(x, g):
    xf = x.astype(jnp.float32)
    y = xf * lax.rsqrt(jnp.mean(xf * xf, axis=-1, keepdims=True) + EPS)
    return (y * g.astype(jnp.float32)).astype(x.dtype)


def layer_norm(x, g, b):
    xf = x.astype(jnp.float32)
    mu = jnp.mean(xf, axis=-1, keepdims=True)
    var = jnp.mean(jnp.square(xf - mu), axis=-1, keepdims=True)
    y = (xf - mu) * lax.rsqrt(var + EPS)
    return (y * g.astype(jnp.float32) + b.astype(jnp.float32)).astype(x.dtype)


def modulate(h, shift, scale):
    return h * (1 + scale[:, None, :]) + shift[:, None, :]


def swiglu(h, w_in, w_out):
    g, u = jnp.split(h @ w_in, 2, axis=-1)
    return (jax.nn.silu(g) * u) @ w_out


def depthwise_conv(x, w, b):
    pad = w.shape[0] // 2
    y = lax.conv_general_dilated(
        x, w[:, None, :].astype(x.dtype), window_strides=(1,), padding=[(pad, pad)],
        dimension_numbers=("NWC", "WIO", "NWC"), feature_group_count=x.shape[-1])
    return y + b.astype(x.dtype)


def hyena_filters_freq(L, w1, b1, w2, b2, w3, freq):
    f32 = jnp.float32
    t = jnp.linspace(0.0, 1.0, L, dtype=f32)
    w = 2.0 * math.pi * jnp.arange(L, dtype=f32) / L
    bands = jnp.linspace(1e-4, HY_BANDS - 1, HY_BANDS, dtype=f32)
    fw = w[:, None] * bands[None, :]
    feats = jnp.concatenate([t[:, None], jnp.cos(fw), -jnp.sin(fw)], axis=-1)
    fr = freq.astype(f32)
    h = jnp.sin(fr * (feats @ w1.astype(f32) + b1.astype(f32)))
    h = jnp.sin(fr * (h @ w2.astype(f32) + b2.astype(f32)))
    h = (h @ w3.astype(f32)).reshape(L, 2, HY_ORDER, D_B)
    max_decay = math.log(HY_TARGET) / HY_FAST_PCT
    min_decay = math.log(HY_TARGET) / HY_SLOW_PCT
    deltas = jnp.abs(jnp.linspace(min_decay, max_decay, D_B, dtype=f32))
    h = h * jnp.exp(-t[:, None] * deltas[None, :])[:, None, None, :]
    fwd, bwd = h[:, 0], h[:, 1]
    k = jnp.concatenate([fwd, jnp.zeros((1, HY_ORDER, D_B), f32), bwd[1:][::-1]], axis=0)
    k = k * lax.rsqrt(jnp.sum(k * k, axis=0, keepdims=True) + EPS)
    return jnp.fft.rfft(k, axis=0)


def fft_long_conv(z, k_f, skip):
    L = z.shape[1]
    zf = z.astype(jnp.float32)
    y = jnp.fft.irfft(jnp.fft.rfft(zf, n=2 * L, axis=1) * k_f[None], n=2 * L, axis=1)[:, :L]
    return (y + zf * skip.astype(jnp.float32)).astype(z.dtype)


def conv_hyena_mixer(h, w_in, conv_w, conv_b, ln_g, ln_b, short_w, short_b,
                     hw1, hb1, hw2, hb2, hw3, hfreq, hskip, w_out):
    L = h.shape[1]
    proj = h @ w_in
    a, bp = proj[..., :2 * D_A], proj[..., 2 * D_A:]
    a = jax.nn.glu(a, axis=-1)
    a = jax.nn.silu(layer_norm(depthwise_conv(a, conv_w, conv_b), ln_g, ln_b))
    bp = depthwise_conv(bp, short_w, short_b)
    v, x1, x2 = jnp.split(bp, 3, axis=-1)
    k_f = hyena_filters_freq(L, hw1, hb1, hw2, hb2, hw3, hfreq)
    z = x1 * fft_long_conv(v, k_f[:, 0], hskip[0])
    z = x2 * fft_long_conv(z, k_f[:, 1], hskip[1])
    return jnp.concatenate([a, z], axis=-1) @ w_out


def alibi_slopes():
    return np.asarray([2.0 ** (-8.0 * (i + 1) / N_HEADS) for i in range(N_HEADS)], np.float32)


def window_attention(h, w_qkv, sink, w_out):
    B, L, _ = h.shape
    nb = L // BLOCK
    qkv = h @ w_qkv
    nq, nk = N_HEADS * HEAD_DIM, N_KV * HEAD_DIM
    q = qkv[..., :nq].reshape(B, nb, BLOCK, N_KV, GROUP, HEAD_DIM)
    k = qkv[..., nq:nq + nk].reshape(B, L, N_KV, HEAD_DIM)
    v = qkv[..., nq + nk:].reshape(B, L, N_KV, HEAD_DIM)

    def band(t):
        tp = jnp.pad(t, ((0, 0), (BLOCK, BLOCK), (0, 0), (0, 0))).reshape(B, nb + 2, BLOCK, N_KV, HEAD_DIM)
        return jnp.concatenate([tp[:, :-2], tp[:, 1:-1], tp[:, 2:]], axis=2)

    kb, vb = band(k), band(v)
    s = jnp.einsum("bnqkgd,bnskd->bnkgqs", q, kb).astype(jnp.float32) * (HEAD_DIM ** -0.5)
    qi = np.arange(BLOCK)[:, None]
    kj = np.arange(3 * BLOCK)[None, :]
    dist = kj - BLOCK - qi
    kpos = (np.arange(nb)[:, None, None] - 1) * BLOCK + kj[None]
    valid = (np.abs(dist)[None] <= WINDOW) & (kpos >= 0) & (kpos < L)
    slopes = jnp.asarray(alibi_slopes()).reshape(N_KV, GROUP)
    bias = -slopes[:, :, None, None] * jnp.asarray(np.abs(dist), jnp.float32)[None, None]
    s = jnp.where(jnp.asarray(valid)[None, :, None, None], s + bias[None, None], NEG)
    snk = sink.astype(jnp.float32).reshape(N_KV, GROUP)[None, None, :, :, None]
    m = jnp.maximum(jnp.max(s, axis=-1), snk)
    p = jnp.exp(s - m[..., None])
    denom = jnp.sum(p, axis=-1) + jnp.exp(snk - m)
    o = jnp.einsum("bnkgqs,bnskd->bnqkgd", p, vb.astype(jnp.float32))
    o = o / jnp.moveaxis(denom, 4, 2)[..., None]
    return o.reshape(B, L, N_HEADS * HEAD_DIM).astype(h.dtype) @ w_out


def trunk(x, c, ada_w, ada_b, norm_g, ffn_w_in, ffn_w_out, final_g,
          ab_w_in, conv_w, conv_b, conv_ln_g, conv_ln_b, hy_short_w, hy_short_b,
          hy_w1, hy_b1, hy_w2, hy_b2, hy_w3, hy_freq, hy_skip, ab_w_out,
          attn_w_qkv, attn_sink, attn_w_out):
    cs = jax.nn.silu(c)
    for i in range(DEPTH):
        mod = cs @ ada_w[i] + ada_b[i]
        sh1, sc1, g1, sh2, sc2, g2, sh3, sc3, g3 = jnp.split(mod, 3 * N_SUB, axis=-1)
        h = modulate(rms_norm(x, norm_g[i, 0]), sh1, sc1)
        x = x + 0.5 * g1[:, None, :] * swiglu(h, ffn_w_in[i, 0], ffn_w_out[i, 0])
        h = modulate(rms_norm(x, norm_g[i, 1]), sh2, sc2)
        j = i // 2
        if i % 2 == 0:
            y = conv_hyena_mixer(h, ab_w_in[j], conv_w[j], conv_b[j], conv_ln_g[j], conv_ln_b[j],
                                 hy_short_w[j], hy_short_b[j], hy_w1[j], hy_b1[j], hy_w2[j], hy_b2[j],
                                 hy_w3[j], hy_freq[j], hy_skip[j], ab_w_out[j])
        else:
            y = window_attention(h, attn_w_qkv[j], attn_sink[j], attn_w_out[j])
        x = x + g2[:, None, :] * y
        h = modulate(rms_norm(x, norm_g[i, 2]), sh3, sc3)
        x = x + 0.5 * g3[:, None, :] * swiglu(h, ffn_w_in[i, 1], ffn_w_out[i, 1])
    return rms_norm(x, final_g)


def setup_inputs(seed: int = 0) -> dict:
    key = jax.random.key(seed)
    ks = iter(jax.random.split(key, 40))
    f32 = jnp.float32

    def nrm(shape, scale):
        return jax.random.normal(next(ks), shape, f32) * scale

    D = D_MODEL
    return {
        "x_prompt": nrm((BATCH, SEQ, D), 1.0),
        "x_sample": nrm((DEC_BATCH, DEC_SEQ, D), 1.0),
        "c_prompt": nrm((BATCH, D), 1.0),
        "c_sample": nrm((DEC_BATCH, D), 1.0),
        "ada_w": nrm((DEPTH, D, 3 * N_SUB * D), 0.02),
        "ada_b": nrm((DEPTH, 3 * N_SUB * D), 0.02),
        "norm_g": 1.0 + nrm((DEPTH, N_SUB, D), 0.05),
        "ffn_w_in": nrm((DEPTH, 2, D, 2 * D_FF), D ** -0.5),
        "ffn_w_out": nrm((DEPTH, 2, D_FF, D), D_FF ** -0.5),
        "final_g": 1.0 + nrm((D,), 0.05),
        "ab_w_in": nrm((N_EVEN, D, D_AB_IN), D ** -0.5),
        "conv_w": nrm((N_EVEN, CONV_W, D_A), CONV_W ** -0.5),
        "conv_b": nrm((N_EVEN, D_A), 0.02),
        "conv_ln_g": 1.0 + nrm((N_EVEN, D_A), 0.05),
        "conv_ln_b": nrm((N_EVEN, D_A), 0.02),
        "hy_short_w": nrm((N_EVEN, HY_SHORT, (HY_ORDER + 1) * D_B), HY_SHORT ** -0.5),
        "hy_short_b": nrm((N_EVEN, (HY_ORDER + 1) * D_B), 0.02),
        "hy_w1": nrm((N_EVEN, HY_EMB, HY_HIDDEN), HY_EMB ** -0.5),
        "hy_b1": nrm((N_EVEN, HY_HIDDEN), 0.1),
        "hy_w2": nrm((N_EVEN, HY_HIDDEN, HY_HIDDEN), HY_HIDDEN ** -0.5),
        "hy_b2": nrm((N_EVEN, HY_HIDDEN), 0.1),
        "hy_w3": nrm((N_EVEN, HY_HIDDEN, 2 * HY_ORDER * D_B), HY_HIDDEN ** -0.5),
        "hy_freq": 1.0 + nrm((N_EVEN, HY_HIDDEN), 0.1),
        "hy_skip": nrm((N_EVEN, HY_ORDER, D_B), 0.1),
        "ab_w_out": nrm((N_EVEN, D_A + D_B, D), (D_A + D_B) ** -0.5),
        "attn_w_qkv": nrm((N_ODD, D, (N_HEADS + 2 * N_KV) * HEAD_DIM), D ** -0.5),
        "attn_sink": nrm((N_ODD, N_HEADS), 0.5),
        "attn_w_out": nrm((N_ODD, N_HEADS * HEAD_DIM, D), (N_HEADS * HEAD_DIM) ** -0.5),
    }


def reference(x_prompt, x_sample, c_prompt, c_sample, ada_w, ada_b, norm_g, ffn_w_in, ffn_w_out,
              final_g, ab_w_in, conv_w, conv_b, conv_ln_g, conv_ln_b, hy_short_w, hy_short_b,
              hy_w1, hy_b1, hy_w2, hy_b2, hy_w3, hy_freq, hy_skip, ab_w_out,
              attn_w_qkv, attn_sink, attn_w_out):
    y_prompt = trunk(x_prompt, c_prompt, ada_w, ada_b, norm_g, ffn_w_in, ffn_w_out, final_g,
                     ab_w_in, conv_w, conv_b, conv_ln_g, conv_ln_b, hy_short_w, hy_short_b,
                     hy_w1, hy_b1, hy_w2, hy_b2, hy_w3, hy_freq, hy_skip, ab_w_out,
                     attn_w_qkv, attn_sink, attn_w_out)
    y_sample = trunk(x_sample, c_sample, ada_w, ada_b, norm_g, ffn_w_in, ffn_w_out, final_g,
                     ab_w_in, conv_w, conv_b, conv_ln_g, conv_ln_b, hy_short_w, hy_short_b,
                     hy_w1, hy_b1, hy_w2, hy_b2, hy_w3, hy_freq, hy_skip, ab_w_out,
                     attn_w_qkv, attn_sink, attn_w_out)
    return (y_prompt, y_sample)
```

```python
import functools
import math

import numpy as np
import jax
import jax.numpy as jnp
from jax import lax
from jax.experimental import pallas as pl
from jax.experimental.pallas import tpu as pltpu

F32 = jnp.float32
BF16 = jnp.bfloat16

D_MODEL = 1024
D_FF = 2816
N_SUB = 3
EPS = 1e-6
D_A = 512
CONV_W = 31
CONV_HALO = 16
D_B = 512
HY_ORDER = 2
HY_BANDS = 16
HY_HIDDEN = 64
HY_FAST_PCT = 0.3
HY_SLOW_PCT = 1.5
HY_TARGET = 1e-2
N_HEADS = 16
N_KV = 4
GROUP = 4
HEAD_DIM = 64
WINDOW = 128
NEG = -1e30

LANES = 128
VMEM_LIMIT = 56 * 1024 * 1024
HI = lax.Precision.HIGHEST


def _params(sem, vmem=VMEM_LIMIT):
    return pltpu.CompilerParams(dimension_semantics=sem, vmem_limit_bytes=vmem)


def _const_spec(shape):
    nd = len(shape)
    return pl.BlockSpec(shape, lambda *_: (0,) * nd, pipeline_mode=pl.Buffered(1))


def _silu(x):
    return x * jax.nn.sigmoid(x)


def _norm_mod(x, g, shift, scale):
    ms = jnp.mean(x * x, axis=-1, keepdims=True)
    return (x * lax.rsqrt(ms + EPS) * g) * (1.0 + scale) + shift


def _ada_kernel(c_ref, w_ref, b_ref, o_ref):
    cs = _silu(c_ref[...])
    o_ref[...] = jnp.dot(cs, w_ref[...], precision=HI, preferred_element_type=F32) + b_ref[...]


def _ada_mod(c_pad, ada_w, ada_b):
    depth, d, n = ada_w.shape
    tn = 1152
    return pl.pallas_call(
        _ada_kernel,
        grid=(depth, n // tn),
        in_specs=[pl.BlockSpec((8, d), lambda l, j: (0, 0)),
                  pl.BlockSpec((None, d, tn), lambda l, j: (l, 0, j)),
                  pl.BlockSpec((None, 1, tn), lambda l, j: (l, 0, j))],
        out_specs=pl.BlockSpec((None, 8, tn), lambda l, j: (l, 0, j)),
        out_shape=jax.ShapeDtypeStruct((depth, 8, n), F32),
        compiler_params=_params(("arbitrary", "arbitrary")),
        name="ada_mod",
    )(c_pad, ada_w, ada_b.reshape(depth, 1, n))


def _ffn_kernel(x_ref, mod_ref, g_ref, win_ref, wout_ref, fg_ref, o_ref, *, chunk, final):
    x = x_ref[...]
    h = _norm_mod(x, g_ref[...], mod_ref[0:1, :], mod_ref[1:2, :]).astype(BF16)
    acc = jnp.zeros(x.shape, F32)
    for j in range(D_FF // chunk):
        gate = jnp.dot(h, win_ref[:, j * chunk:(j + 1) * chunk], preferred_element_type=F32)
        up = jnp.dot(h, win_ref[:, D_FF + j * chunk:D_FF + (j + 1) * chunk], preferred_element_type=F32)
        act = (_silu(gate) * up).astype(BF16)
        acc = acc + jnp.dot(act, wout_ref[j * chunk:(j + 1) * chunk, :], preferred_element_type=F32)
    y = x + (0.5 * mod_ref[2:3, :]) * acc
    if final:
        fg = fg_ref[...]
        y = y * lax.rsqrt(jnp.mean(y * y, axis=-1, keepdims=True) + EPS) * fg
    o_ref[...] = y


def _ffn(x, mod3, norm_g, w_in, w_out, final_g, *, final, tm=512, chunk=1408):
    b, l, d = x.shape
    kern = functools.partial(_ffn_kernel, chunk=chunk, final=final)
    return pl.pallas_call(
        kern,
        grid=(b, l // tm),
        in_specs=[pl.BlockSpec((None, tm, d), lambda bi, i: (bi, i, 0)),
                  pl.BlockSpec((None, 3, d), lambda bi, i: (bi, 0, 0)),
                  _const_spec((1, d)),
                  _const_spec(w_in.shape),
                  _const_spec(w_out.shape),
                  _const_spec((1, d))],
        out_specs=pl.BlockSpec((None, tm, d), lambda bi, i: (bi, i, 0)),
        out_shape=jax.ShapeDtypeStruct(x.shape, F32),
        compiler_params=_params(("arbitrary", "arbitrary")),
        name="ffn",
    )(x, mod3, norm_g.reshape(1, d), w_in, w_out, final_g.reshape(1, d))


def _mix_in_kernel(xp_ref, x_ref, xn_ref, mod_ref, g_ref, wa_ref, wbt_ref, cw_ref, cb_ref, lg_ref, lb_ref,
                   a_ref, bpt_ref, aext_ref, *, tm):
    i = pl.program_id(1)
    last = pl.num_programs(1) - 1
    g, shift, scale = g_ref[...], mod_ref[0:1, :], mod_ref[1:2, :]

    def glu_rows(xr):
        hr = _norm_mod(xr, g, shift, scale).astype(BF16)
        pa = jnp.dot(hr, wa_ref[...], preferred_element_type=F32)
        return hr, pa[:, :D_A] * jax.nn.sigmoid(pa[:, D_A:])

    h, a_cur = glu_rows(x_ref[...])
    _, a_prev = glu_rows(xp_ref[...])
    _, a_next = glu_rows(xn_ref[...])
    aext_ref[0:CONV_HALO, :] = jnp.where(i > 0, a_prev, 0.0)
    aext_ref[CONV_HALO:CONV_HALO + tm, :] = a_cur
    aext_ref[CONV_HALO + tm:, :] = jnp.where(i < last, a_next, 0.0)

    acc = jnp.zeros((tm, D_A), F32) + cb_ref[...]
    off = CONV_HALO - CONV_W // 2
    for j in range(CONV_W):
        acc = acc + cw_ref[j:j + 1, :] * aext_ref[off + j:off + j + tm, :]
    mu = jnp.mean(acc, axis=-1, keepdims=True)
    cen = acc - mu
    var = jnp.mean(cen * cen, axis=-1, keepdims=True)
    a_ref[...] = _silu(cen * lax.rsqrt(var + EPS) * lg_ref[...] + lb_ref[...]).astype(a_ref.dtype)

    bpt_ref[...] = lax.dot_general(wbt_ref[...], h, (((1,), (1,)), ((), ())), preferred_element_type=F32)


def _mix_in(x, mod3, norm_g, wa, wbt, conv_w, conv_b, ln_g, ln_b, *, tm=512):
    b, l, d = x.shape
    nh = tm // CONV_HALO
    nblk = l // CONV_HALO
    kern = functools.partial(_mix_in_kernel, tm=tm)
    return pl.pallas_call(
        kern,
        grid=(b, l // tm),
        in_specs=[pl.BlockSpec((None, CONV_HALO, d), lambda bi, i: (bi, jnp.maximum(i * nh - 1, 0), 0)),
                  pl.BlockSpec((None, tm, d), lambda bi, i: (bi, i, 0)),
                  pl.BlockSpec((None, CONV_HALO, d), lambda bi, i: (bi, jnp.minimum((i + 1) * nh, nblk - 1), 0)),
                  pl.BlockSpec((None, 3, d), lambda bi, i: (bi, 0, 0)),
                  _const_spec((1, d)),
                  _const_spec(wa.shape),
                  _const_spec(wbt.shape),
                  _const_spec(conv_w.shape),
                  _const_spec((1, D_A)), _const_spec((1, D_A)), _const_spec((1, D_A))],
        out_specs=[pl.BlockSpec((None, tm, D_A), lambda bi, i: (bi, i, 0)),
                   pl.BlockSpec((None, 3 * D_B, tm), lambda bi, i: (bi, 0, i))],
        out_shape=[jax.ShapeDtypeStruct((b, l, D_A), BF16),
                   jax.ShapeDtypeStruct((b, 3 * D_B, l), F32)],
        scratch_shapes=[pltpu.VMEM((tm + 2 * CONV_HALO, D_A), F32)],
        compiler_params=_params(("arbitrary", "arbitrary")),
        name="mix_in",
    )(x, x, x, mod3, norm_g.reshape(1, d), wa, wbt, conv_w, conv_b.reshape(1, D_A),
      ln_g.reshape(1, D_A), ln_b.reshape(1, D_A))


def _fft_tables(l):
    n = 2 * l
    n1 = n // LANES
    k1 = np.arange(n1)[:, None].astype(np.float64)
    r1 = np.arange(n1)[None, :].astype(np.float64)
    ang1 = 2.0 * np.pi * k1 * r1 / n1
    f1 = np.concatenate([np.cos(ang1), -np.sin(ang1)], axis=0)
    n2 = np.arange(LANES)[None, :].astype(np.float64)
    angt = 2.0 * np.pi * k1 * n2 / n
    tc = np.concatenate([np.cos(angt), np.cos(angt)], axis=1)
    ts = np.concatenate([np.sin(angt), -np.sin(angt)], axis=1)
    a2 = 2.0 * np.pi * np.arange(LANES)[:, None] * np.arange(LANES)[None, :] / LANES
    c2, s2 = np.cos(a2), -np.sin(a2)
    f2f = np.block([[c2, s2], [-s2, c2]])
    f2i = np.block([[c2, -s2], [s2, c2]])
    ang4 = 2.0 * np.pi * np.arange(n1 // 2)[:, None] * np.arange(n1)[None, :] / n1
    g4 = np.concatenate([np.cos(ang4), -np.sin(ang4)], axis=1)
    return dict(
        f1=jnp.asarray(f1, BF16), f1h=jnp.asarray(f1[:, :n1 // 2], BF16),
        tc=jnp.asarray(tc, F32), ts=jnp.asarray(ts, F32),
        f2f=jnp.asarray(f2f, BF16), f2i=jnp.asarray(f2i, BF16), g4=jnp.asarray(g4, BF16))


def _swap(a):
    return jnp.concatenate([a[..., LANES:], a[..., :LANES]], axis=-1)


def _stage1_pair(f1_ref, z0, z1, tc, ts):
    zp = jnp.concatenate([z0, z1], axis=1).astype(BF16)
    aa = jnp.dot(f1_ref[...], zp, preferred_element_type=F32)
    n1 = aa.shape[0] // 2
    outs = []
    for q in range(2):
        a = jnp.concatenate([aa[:n1, q * LANES:(q + 1) * LANES], aa[n1:, q * LANES:(q + 1) * LANES]], axis=1)
        outs.append(a * tc + _swap(a) * ts)
    return outs


def _filt_gen_kernel(w1_ref, b1_ref, w2_ref, b2_ref, fr_ref, w3t_ref, dl_ref, k_ref, ssq_ref, *, l, tt):
    i = pl.program_id(0)
    inv_lm1 = 1.0 / (l - 1)

    def pos(shape, axis):
        n = i * tt + lax.broadcasted_iota(jnp.int32, shape, axis)
        return n, jnp.where(n < l, n, 2 * l - n).astype(F32)

    _, jcol = pos((tt, 1), 0)
    t = jcol * inv_lm1
    w = (2.0 * math.pi) * jcol / l
    bands = 1e-4 + lax.broadcasted_iota(jnp.int32, (1, HY_BANDS), 1).astype(F32) * ((HY_BANDS - 1 - 1e-4) / (HY_BANDS - 1))
    fw = w * bands
    fr = fr_ref[...]
    z1 = (t * w1_ref[0:1, :]
          + jnp.dot(jnp.cos(fw), w1_ref[1:1 + HY_BANDS, :], precision=HI, preferred_element_type=F32)
          - jnp.dot(jnp.sin(fw), w1_ref[1 + HY_BANDS:, :], precision=HI, preferred_element_type=F32)
          + b1_ref[...])
    h1 = jnp.sin(fr * z1)
    h2 = jnp.sin(fr * (jnp.dot(h1, w2_ref[...], precision=HI, preferred_element_type=F32) + b2_ref[...]))
    ht = lax.dot_general(w3t_ref[...], h2, (((1,), (1,)), ((), ())), precision=HI,
                         preferred_element_type=F32)
    nrow, jrow = pos((1, tt), 1)
    dec = jnp.exp(-(jrow * inv_lm1) * dl_ref[...])
    dec = jnp.where(nrow == l, 0.0, dec)
    k = ht * jnp.concatenate([dec] * HY_ORDER, axis=0)
    k_ref[...] = k

    @pl.when(i == 0)
    def _():
        ssq_ref[...] = jnp.zeros_like(ssq_ref)

    k2 = k * k
    part = k2[:, 0:LANES]
    for q in range(1, tt // LANES):
        part = part + k2[:, q * LANES:(q + 1) * LANES]
    ssq_ref[...] += part


def _filt_gen(hw1, hb1, hw2, hb2, hw3, hfreq, l, *, tt=1024):
    n = 2 * l
    nt = n // tt
    rows = HY_ORDER * D_B
    w3t = hw3.T.reshape(2, rows, HY_HIDDEN)
    max_decay = math.log(HY_TARGET) / HY_FAST_PCT
    min_decay = math.log(HY_TARGET) / HY_SLOW_PCT
    deltas = jnp.abs(jnp.linspace(min_decay, max_decay, D_B, dtype=F32)).reshape(D_B, 1)
    kern = functools.partial(_filt_gen_kernel, l=l, tt=tt)
    return pl.pallas_call(
        kern,
        grid=(nt,),
        in_specs=[_const_spec(hw1.shape), _const_spec((1, HY_HIDDEN)), _const_spec(hw2.shape),
                  _const_spec((1, HY_HIDDEN)), _const_spec((1, HY_HIDDEN)),
                  pl.BlockSpec((None, rows, HY_HIDDEN), lambda i: (jnp.where(i * tt >= l, 1, 0), 0, 0)),
                  _const_spec((D_B, 1))],
        out_specs=[pl.BlockSpec((rows, tt), lambda i: (0, i)),
                   pl.BlockSpec((rows, LANES), lambda i: (0, 0))],
        out_shape=[jax.ShapeDtypeStruct((rows, n), F32), jax.ShapeDtypeStruct((rows, LANES), F32)],
        compiler_params=_params(("arbitrary",)),
        name="hyena_filter_gen",
    )(hw1, hb1.reshape(1, -1), hw2, hb2.reshape(1, -1), hfreq.reshape(1, -1), w3t, deltas)


def _filt_fft_kernel(k_ref, ssq_ref, f1_ref, tc_ref, ts_ref, f2_ref, o_ref, b_ref, *, ct, n_total):
    tc, ts = tc_ref[...], ts_ref[...]

    def body(p, carry):
        b0, b1 = _stage1_pair(f1_ref, k_ref[2 * p], k_ref[2 * p + 1], tc, ts)
        b_ref[2 * p] = b0.astype(BF16)
        b_ref[2 * p + 1] = b1.astype(BF16)
        return carry

    lax.fori_loop(0, ct // 2, body, 0)
    n1 = b_ref.shape[1]
    x = jnp.dot(b_ref[...].reshape(ct * n1, 2 * LANES), f2_ref[...], preferred_element_type=F32)
    tot = jnp.sum(ssq_ref[...], axis=-1, keepdims=True)
    scale = lax.rsqrt(tot + EPS) * (1.0 / n_total)
    o_ref[...] = x.reshape(ct, n1, 2 * LANES) * scale


def _filt_fft(k, ssq, tabs, l, *, ct=8):
    rows = k.shape[0]
    n = 2 * l
    n1 = n // LANES
    kern = functools.partial(_filt_fft_kernel, ct=ct, n_total=n)
    return pl.pallas_call(
        kern,
        grid=(rows // ct,),
        in_specs=[pl.BlockSpec((ct, n1, LANES), lambda c: (c, 0, 0)),
                  pl.BlockSpec((ct, 1, LANES), lambda c: (c, 0, 0)),
                  _const_spec(tabs["f1"].shape), _const_spec(tabs["tc"].shape), _const_spec(tabs["ts"].shape),
                  _const_spec(tabs["f2f"].shape)],
        out_specs=pl.BlockSpec((ct, n1, 2 * LANES), lambda c: (c, 0, 0)),
        out_shape=jax.ShapeDtypeStruct((rows, n1, 2 * LANES), F32),
        scratch_shapes=[pltpu.VMEM((ct, n1, 2 * LANES), BF16)],
        compiler_params=_params(("arbitrary",)),
        name="hyena_filter_fft",
    )(k.reshape(rows, n1, LANES), ssq.reshape(rows, 1, LANES), tabs["f1"], tabs["tc"], tabs["ts"], tabs["f2f"])


def _shift_prev(a, row, lane):
    r = pltpu.roll(a, 1, axis=1)
    r2 = pltpu.roll(r, 1, axis=0)
    out = jnp.where(lane == 0, r2, r)
    return jnp.where((lane == 0) & (row == 0), 0.0, out)


def _shift_next(a, row, lane):
    last_l, last_r = a.shape[1] - 1, a.shape[0] - 1
    r = pltpu.roll(a, last_l, axis=1)
    r2 = pltpu.roll(r, last_r, axis=0)
    out = jnp.where(lane == last_l, r2, r)
    return jnp.where((lane == last_l) & (row == last_r), 0.0, out)


def _hyena_kernel(sw_ref, sb_ref, skip_ref, bp_ref, kf_ref, f1_ref, tc_ref, ts_ref, f2f_ref, f2i_ref, g4_ref,
                  o_ref, v_ref, x1_ref, x2_ref, y_ref, b_ref, *, ct):
    ci = pl.program_id(1)
    nh = v_ref.shape[1]
    n1 = 2 * nh
    row = lax.broadcasted_iota(jnp.int32, (nh, LANES), 0)
    lane = lax.broadcasted_iota(jnp.int32, (nh, LANES), 1)

    def short_conv(c, carry):
        cg = ci * ct + c
        for part, dst in enumerate((v_ref, x1_ref, x2_ref)):
            ch = part * D_B + cg
            a = bp_ref[part, c]
            dst[c] = (sw_ref[0, ch] * _shift_prev(a, row, lane) + sw_ref[1, ch] * a
                      + sw_ref[2, ch] * _shift_next(a, row, lane) + sb_ref[ch])
        return carry

    lax.fori_loop(0, ct, short_conv, 0)
    tc, ts = tc_ref[...], ts_ref[...]

    def long_conv(src_ref, order):
        def s1(p, carry):
            b0, b1 = _stage1_pair(f1_ref, src_ref[2 * p], src_ref[2 * p + 1], tc, ts)
            b_ref[2 * p] = b0.astype(BF16)
            b_ref[2 * p + 1] = b1.astype(BF16)
            return carry

        lax.fori_loop(0, ct // 2, s1, 0)
        x = jnp.dot(b_ref[...].reshape(ct * n1, 2 * LANES), f2f_ref[...], preferred_element_type=F32)
        x = x.reshape(ct, n1, 2 * LANES)
        kf = kf_ref[order]
        kr, ki = kf[..., :LANES], kf[..., LANES:]
        y = x * jnp.concatenate([kr, kr], axis=-1) + _swap(x) * jnp.concatenate([-ki, ki], axis=-1)
        cm = jnp.dot(y.astype(BF16).reshape(ct * n1, 2 * LANES), f2i_ref[...], preferred_element_type=F32)
        cm = cm.reshape(ct, n1, 2 * LANES)
        b_ref[...] = (cm * tc[None] - _swap(cm) * ts[None]).astype(BF16)

        def s4(p, carry):
            d0, d1 = b_ref[2 * p], b_ref[2 * p + 1]
            rhs = jnp.concatenate(
                [jnp.concatenate([d0[:, :LANES], d1[:, :LANES]], axis=1),
                 jnp.concatenate([d0[:, LANES:], d1[:, LANES:]], axis=1)], axis=0)
            yy = jnp.dot(g4_ref[...], rhs, preferred_element_type=F32)
            y_ref[2 * p] = yy[:, :LANES]
            y_ref[2 * p + 1] = yy[:, LANES:]
            return carry

        lax.fori_loop(0, ct // 2, s4, 0)

    def gate(c, order, src_ref, mul_ref, dst_ref):
        z = src_ref[c]
        dst_ref[c] = (mul_ref[c] * (y_ref[c] + z * skip_ref[order, ci * ct + c])).astype(dst_ref.dtype)

    long_conv(v_ref, 0)
    lax.fori_loop(0, ct, lambda c, carry: (gate(c, 0, v_ref, x1_ref, v_ref), carry)[1], 0)
    long_conv(v_ref, 1)
    lax.fori_loop(0, ct, lambda c, carry: (gate(c, 1, v_ref, x2_ref, o_ref), carry)[1], 0)


def _hyena(bpt, kf, short_w, short_b, skip, tabs, *, ct=8):
    b, _, l = bpt.shape
    nh = l // LANES
    n1 = 2 * nh
    smem = pl.BlockSpec(memory_space=pltpu.SMEM)
    kern = functools.partial(_hyena_kernel, ct=ct)
    out = pl.pallas_call(
        kern,
        grid=(b, D_B // ct),
        in_specs=[smem, smem, smem,
                  pl.BlockSpec((None, 3, ct, nh, LANES), lambda bi, c: (bi, 0, c, 0, 0)),
                  pl.BlockSpec((HY_ORDER, ct, n1, 2 * LANES), lambda bi, c: (0, c, 0, 0)),
                  _const_spec(tabs["f1h"].shape), _const_spec(tabs["tc"].shape), _const_spec(tabs["ts"].shape),
                  _const_spec(tabs["f2f"].shape), _const_spec(tabs["f2i"].shape), _const_spec(tabs["g4"].shape)],
        out_specs=pl.BlockSpec((None, ct, nh, LANES), lambda bi, c: (bi, c, 0, 0)),
        out_shape=jax.ShapeDtypeStruct((b, D_B, nh, LANES), BF16),
        scratch_shapes=[pltpu.VMEM((ct, nh, LANES), F32), pltpu.VMEM((ct, nh, LANES), F32),
                        pltpu.VMEM((ct, nh, LANES), F32), pltpu.VMEM((ct, nh, LANES), F32),
                        pltpu.VMEM((ct, n1, 2 * LANES), BF16)],
        compiler_params=_params(("arbitrary", "arbitrary")),
        name="hyena_conv",
    )(short_w, short_b, skip, bpt.reshape(b, 3, D_B, nh, LANES), kf.reshape(HY_ORDER, D_B, n1, 2 * LANES),
      tabs["f1h"], tabs["tc"], tabs["ts"], tabs["f2f"], tabs["f2i"], tabs["g4"])
    return out.reshape(b, D_B, l)


def _mix_out_kernel(x_ref, a_ref, zt_ref, mod_ref, woa_ref, woz_ref, o_ref):
    y = jnp.dot(a_ref[...], woa_ref[...], preferred_element_type=F32)
    y = y + lax.dot_general(zt_ref[...], woz_ref[...], (((0,), (0,)), ((), ())), preferred_element_type=F32)
    o_ref[...] = x_ref[...] + mod_ref[2:3, :] * y


def _mix_out(x, a, zt, mod3, woa, woz, *, tm=512):
    b, l, d = x.shape
    return pl.pallas_call(
        _mix_out_kernel,
        grid=(b, l // tm),
        in_specs=[pl.BlockSpec((None, tm, d), lambda bi, i: (bi, i, 0)),
                  pl.BlockSpec((None, tm, D_A), lambda bi, i: (bi, i, 0)),
                  pl.BlockSpec((None, D_B, tm), lambda bi, i: (bi, 0, i)),
                  pl.BlockSpec((None, 3, d), lambda bi, i: (bi, 0, 0)),
                  _const_spec(woa.shape), _const_spec(woz.shape)],
        out_specs=pl.BlockSpec((None, tm, d), lambda bi, i: (bi, i, 0)),
        out_shape=jax.ShapeDtypeStruct(x.shape, F32),
        compiler_params=_params(("arbitrary", "arbitrary")),
        name="mix_out",
    )(x, a, zt, mod3, woa, woz)


def _qkv_kernel(x_ref, mod_ref, g_ref, w_ref, q_ref, k_ref, v_ref):
    h = _norm_mod(x_ref[...], g_ref[...], mod_ref[0:1, :], mod_ref[1:2, :]).astype(BF16)
    qkv = jnp.dot(h, w_ref[...], preferred_element_type=F32)
    nq, nk = N_HEADS * HEAD_DIM, N_KV * HEAD_DIM
    q_ref[...] = (qkv[:, :nq] * (HEAD_DIM ** -0.5)).astype(BF16)

    def with_swapped(t):
        sw = [pltpu.roll(t[:, j * LANES:(j + 1) * LANES], HEAD_DIM, axis=1) for j in range(nk // LANES)]
        return jnp.concatenate([t] + sw, axis=1).astype(BF16)

    k_ref[...] = with_swapped(qkv[:, nq:nq + nk])
    v_ref[...] = with_swapped(qkv[:, nq + nk:])


def _qkv(x, mod3, norm_g, w_qkv, *, tm=512):
    b, l, d = x.shape
    nq, nk = N_HEADS * HEAD_DIM, N_KV * HEAD_DIM
    return pl.pallas_call(
        _qkv_kernel,
        grid=(b, l // tm),
        in_specs=[pl.BlockSpec((None, tm, d), lambda bi, i: (bi, i, 0)),
                  pl.BlockSpec((None, 3, d), lambda bi, i: (bi, 0, 0)),
                  _const_spec((1, d)), _const_spec(w_qkv.shape)],
        out_specs=[pl.BlockSpec((None, tm, nq), lambda bi, i: (bi, i, 0)),
                   pl.BlockSpec((None, tm, 2 * nk), lambda bi, i: (bi, i, 0)),
                   pl.BlockSpec((None, tm, 2 * nk), lambda bi, i: (bi, i, 0))],
        out_shape=[jax.ShapeDtypeStruct((b, l, nq), BF16),
                   jax.ShapeDtypeStruct((b, l, 2 * nk), BF16),
                   jax.ShapeDtypeStruct((b, l, 2 * nk), BF16)],
        compiler_params=_params(("arbitrary", "arbitrary")),
        name="attn_qkv",
    )(x, mod3, norm_g.reshape(1, d), w_qkv)


def _alibi_slopes():
    return [2.0 ** (-8.0 * (i + 1) / N_HEADS) for i in range(N_HEADS)]


def _attn_kernel(sink_ref, x_ref, q_ref, kp_ref, kc_ref, kn_ref, vp_ref, vc_ref, vn_ref, mod_ref, wo_ref,
                 o_ref, kall_ref, vall_ref, oh_ref, *, tq, l):
    i = pl.program_id(1)
    nk = N_KV * HEAD_DIM
    kall_ref[0:WINDOW, :] = kp_ref[...]
    kall_ref[WINDOW:WINDOW + tq, :] = kc_ref[...]
    kall_ref[WINDOW + tq:, :] = kn_ref[...]
    vall_ref[0:WINDOW, :] = vp_ref[...]
    vall_ref[WINDOW:WINDOW + tq, :] = vc_ref[...]
    vall_ref[WINDOW + tq:, :] = vn_ref[...]

    span = 3 * WINDOW
    qi = lax.broadcasted_iota(jnp.int32, (WINDOW, span), 0)
    kj = lax.broadcasted_iota(jnp.int32, (WINDOW, span), 1)
    dist_i = jnp.abs(kj - WINDOW - qi)
    dist = dist_i.astype(F32)
    lane = lax.broadcasted_iota(jnp.int32, (WINDOW, LANES), 1)
    low = lane < HEAD_DIM
    slopes = _alibi_slopes()

    def sub_block(sb, carry):
        r0 = pl.multiple_of(sb * WINDOW, WINDOW)
        kpos = i * tq + r0 - WINDOW + kj
        valid = (dist_i <= WINDOW) & (kpos >= 0) & (kpos < l)
        for pair in range(N_HEADS // 2):
            kvh = pair // 2
            tile = kvh // 2
            q2 = q_ref[pl.ds(r0, WINDOW), pair * LANES:(pair + 1) * LANES]
            qz = jnp.zeros_like(q2)
            q_lo, q_hi = jnp.where(low, q2, qz), jnp.where(low, qz, q2)
            plain = slice(tile * LANES, (tile + 1) * LANES)
            swapped = slice(nk + tile * LANES, nk + (tile + 1) * LANES)
            c_lo, c_hi = (plain, swapped) if kvh % 2 == 0 else (swapped, plain)
            outs = []
            for hh, (qm, cols) in enumerate(((q_lo, c_lo), (q_hi, c_hi))):
                head = 2 * pair + hh
                kw = kall_ref[pl.ds(r0, span), cols]
                vw = vall_ref[pl.ds(r0, span), cols]
                s = lax.dot_general(qm, kw, (((1,), (1,)), ((), ())), preferred_element_type=F32)
                s = jnp.where(valid, s - slopes[head] * dist, NEG)
                snk = sink_ref[head]
                m = jnp.maximum(jnp.max(s, axis=-1, keepdims=True), snk)
                p = jnp.exp(s - m)
                den = jnp.sum(p, axis=-1, keepdims=True) + jnp.exp(snk - m)
                outs.append(jnp.dot(p.astype(BF16), vw, preferred_element_type=F32) / den)
            oh_ref[pl.ds(r0, WINDOW), pair * LANES:(pair + 1) * LANES] = jnp.where(low, outs[0], outs[1]).astype(BF16)
        return carry

    lax.fori_loop(0, tq // WINDOW, sub_block, 0)
    y = jnp.dot(oh_ref[...], wo_ref[...], preferred_element_type=F32)
    o_ref[...] = x_ref[...] + mod_ref[2:3, :] * y


def _attn(x, q, kk, vv, mod3, sink, w_out, *, tq=512):
    b, l, d = x.shape
    nq, nk2 = q.shape[-1], kk.shape[-1]
    r = tq // WINDOW
    nblk = l // WINDOW
    prev = lambda bi, i: (bi, jnp.maximum(i * r - 1, 0), 0)
    cur = lambda bi, i: (bi, i, 0)
    nxt = lambda bi, i: (bi, jnp.minimum((i + 1) * r, nblk - 1), 0)
    kern = functools.partial(_attn_kernel, tq=tq, l=l)
    return pl.pallas_call(
        kern,
        grid=(b, l // tq),
        in_specs=[pl.BlockSpec(memory_space=pltpu.SMEM),
                  pl.BlockSpec((None, tq, d), cur),
                  pl.BlockSpec((None, tq, nq), cur),
                  pl.BlockSpec((None, WINDOW, nk2), prev), pl.BlockSpec((None, tq, nk2), cur),
                  pl.BlockSpec((None, WINDOW, nk2), nxt),
                  pl.BlockSpec((None, WINDOW, nk2), prev), pl.BlockSpec((None, tq, nk2), cur),
                  pl.BlockSpec((None, WINDOW, nk2), nxt),
                  pl.BlockSpec((None, 3, d), lambda bi, i: (bi, 0, 0)),
                  _const_spec(w_out.shape)],
        out_specs=pl.BlockSpec((None, tq, d), cur),
        out_shape=jax.ShapeDtypeStruct(x.shape, F32),
        scratch_shapes=[pltpu.VMEM((tq + 2 * WINDOW, nk2), BF16), pltpu.VMEM((tq + 2 * WINDOW, nk2), BF16),
                        pltpu.VMEM((tq, nq), BF16)],
        compiler_params=_params(("arbitrary", "arbitrary")),
        name="attn",
    )(sink, x, q, kk, kk, kk, vv, vv, vv, mod3, w_out)


def _trunk(x, mods, w, filt, tabs):
    m = [mods[0][:, 3 * s:3 * s + 3] for s in range(N_SUB)]
    x = _ffn(x, m[0], w["norm_g"][0, 0], w["ffn_in"][0][0], w["ffn_out"][0][0], w["final_g"], final=False)
    a, bpt = _mix_in(x, m[1], w["norm_g"][0, 1], w["ab_wa"], w["ab_wbt"], w["conv_w"], w["conv_b"],
                     w["conv_ln_g"], w["conv_ln_b"])
    zt = _hyena(bpt, filt, w["hy_short_w"], w["hy_short_b"], w["hy_skip"], tabs)
    x = _mix_out(x, a, zt, m[1], w["ab_woa"], w["ab_woz"])
    x = _ffn(x, m[2], w["norm_g"][0, 2], w["ffn_in"][0][1], w["ffn_out"][0][1], w["final_g"], final=False)

    m = [mods[1][:, 3 * s:3 * s + 3] for s in range(N_SUB)]
    x = _ffn(x, m[0], w["norm_g"][1, 0], w["ffn_in"][1][0], w["ffn_out"][1][0], w["final_g"], final=False)
    q, kk, vv = _qkv(x, m[1], w["norm_g"][1, 1], w["attn_w_qkv"])
    x = _attn(x, q, kk, vv, m[1], w["attn_sink"], w["attn_w_out"])
    x = _ffn(x, m[2], w["norm_g"][1, 2], w["ffn_in"][1][1], w["ffn_out"][1][1], w["final_g"], final=True)
    return x


def kernel(x_prompt, x_sample, c_prompt, c_sample, ada_w, ada_b, norm_g, ffn_w_in, ffn_w_out, final_g, ab_w_in,
           conv_w, conv_b, conv_ln_g, conv_ln_b, hy_short_w, hy_short_b, hy_w1, hy_b1, hy_w2, hy_b2, hy_w3,
           hy_freq, hy_skip, ab_w_out, attn_w_qkv, attn_sink, attn_w_out):
    l = x_prompt.shape[1]
    assert x_sample.shape[1] == l and l % 512 == 0
    bp, bs = x_prompt.shape[0], x_sample.shape[0]
    c_pad = jnp.zeros((8, D_MODEL), F32).at[:bp].set(c_prompt).at[bp:bp + bs].set(c_sample)
    mod = _ada_mod(c_pad, ada_w, ada_b)
    depth = mod.shape[0]
    mod = mod.reshape(depth, 8, 3 * N_SUB, D_MODEL)

    w = dict(
        norm_g=norm_g, final_g=final_g,
        ffn_in=[[ffn_w_in[i, s].astype(BF16) for s in range(2)] for i in range(depth)],
        ffn_out=[[ffn_w_out[i, s].astype(BF16) for s in range(2)] for i in range(depth)],
        ab_wa=ab_w_in[0, :, :2 * D_A].astype(BF16),
        ab_wbt=ab_w_in[0, :, 2 * D_A:].T.astype(BF16),
        conv_w=conv_w[0], conv_b=conv_b[0], conv_ln_g=conv_ln_g[0], conv_ln_b=conv_ln_b[0],
        hy_short_w=hy_short_w[0], hy_short_b=hy_short_b[0], hy_skip=hy_skip[0],
        ab_woa=ab_w_out[0, :D_A].astype(BF16), ab_woz=ab_w_out[0, D_A:].astype(BF16),
        attn_w_qkv=attn_w_qkv[0].astype(BF16), attn_sink=attn_sink[0], attn_w_out=attn_w_out[0].astype(BF16),
    )
    tabs = _fft_tables(l)
    kt, ssq = _filt_gen(hy_w1[0], hy_b1[0], hy_w2[0], hy_b2[0], hy_w3[0], hy_freq[0], l)
    filt = _filt_fft(kt, ssq, tabs, l)

    y_prompt = _trunk(x_prompt, mod[:, :bp], w, filt, tabs)
    y_sample = _trunk(x_sample, mod[:, bp:bp + bs], w, filt, tabs)
    return (y_prompt, y_sample)
```

```python
import functools
import math

import numpy as np
import jax
import jax.numpy as jnp
from jax import lax
from jax.experimental import pallas as pl
from jax.experimental.pallas import tpu as pltpu

F32 = jnp.float32
BF16 = jnp.bfloat16

D_MODEL = 1024
D_FF = 2816
N_SUB = 3
EPS = 1e-6
D_A = 512
CONV_W = 31
CONV_HALO = 16
D_B = 512
HY_ORDER = 2
HY_BANDS = 16
HY_HIDDEN = 64
HY_FAST_PCT = 0.3
HY_SLOW_PCT = 1.5
HY_TARGET = 1e-2
N_HEADS = 16
N_KV = 4
GROUP = 4
HEAD_DIM = 64
WINDOW = 128
NEG = -1e30
LOG2E = math.log2(math.e)

SUBLANES = 8
LANES = 128
VMEM_LIMIT = 56 * 1024 * 1024
HI = lax.Precision.HIGHEST


def _params(sem, vmem=VMEM_LIMIT):
    return pltpu.CompilerParams(dimension_semantics=sem, vmem_limit_bytes=vmem)


def _const_spec(shape):
    nd = len(shape)
    return pl.BlockSpec(shape, lambda *_: (0,) * nd, pipeline_mode=pl.Buffered(1))


def _silu(x):
    return x * jax.nn.sigmoid(x)


def _norm_mod(x, g, shift, scale):
    ms = jnp.mean(x * x, axis=-1, keepdims=True)
    return (x * lax.rsqrt(ms + EPS) * g) * (1.0 + scale) + shift


def _ada_kernel(c_ref, w_ref, b_ref, o_ref):
    cs = _silu(c_ref[...])
    o_ref[...] = jnp.dot(cs, w_ref[...], precision=HI, preferred_element_type=F32) + b_ref[...]


def _ada_mod(c_pad, ada_w, ada_b):
    depth, d, n = ada_w.shape
    tn = 1152
    return pl.pallas_call(
        _ada_kernel,
        grid=(depth, n // tn),
        in_specs=[pl.BlockSpec((8, d), lambda l, j: (0, 0)),
                  pl.BlockSpec((None, d, tn), lambda l, j: (l, 0, j)),
                  pl.BlockSpec((None, 1, tn), lambda l, j: (l, 0, j))],
        out_specs=pl.BlockSpec((None, 8, tn), lambda l, j: (l, 0, j)),
        out_shape=jax.ShapeDtypeStruct((depth, 8, n), F32),
        compiler_params=_params(("arbitrary", "arbitrary")),
        name="ada_mod",
    )(c_pad, ada_w, ada_b.reshape(depth, 1, n))


def _ffn_kernel(x_ref, mod_ref, g_ref, win_ref, wout_ref, fg_ref, o_ref, *, final, halves):
    rows = x_ref.shape[0] // halves
    for hf in range(halves):
        rs = slice(hf * rows, (hf + 1) * rows)
        x = x_ref[rs, :]
        h = _norm_mod(x, g_ref[...], mod_ref[0:1, :], mod_ref[1:2, :]).astype(BF16)
        gate = jnp.dot(h, win_ref[:, :D_FF], preferred_element_type=F32)
        up = jnp.dot(h, win_ref[:, D_FF:], preferred_element_type=F32)
        act = (_silu(gate) * up).astype(BF16)
        y = x + (0.5 * mod_ref[2:3, :]) * jnp.dot(act, wout_ref[...], preferred_element_type=F32)
        if final:
            y = y * lax.rsqrt(jnp.mean(y * y, axis=-1, keepdims=True) + EPS) * fg_ref[...]
        o_ref[rs, :] = y


def _ffn(x, mod3, norm_g, w_in, w_out, final_g, *, final, tm=1024, halves=2):
    b, l, d = x.shape
    kern = functools.partial(_ffn_kernel, final=final, halves=halves)
    return pl.pallas_call(
        kern,
        grid=(b, l // tm),
        in_specs=[pl.BlockSpec((None, tm, d), lambda bi, i: (bi, i, 0)),
                  pl.BlockSpec((None, 3, d), lambda bi, i: (bi, 0, 0)),
                  _const_spec((1, d)),
                  _const_spec(w_in.shape),
                  _const_spec(w_out.shape),
                  _const_spec((1, d))],
        out_specs=pl.BlockSpec((None, tm, d), lambda bi, i: (bi, i, 0)),
        out_shape=jax.ShapeDtypeStruct(x.shape, F32),
        compiler_params=_params(("arbitrary", "arbitrary")),
        name="ffn",
    )(x, mod3, norm_g.reshape(1, d), w_in, w_out, final_g.reshape(1, d))


def _mix_in_kernel(xp_ref, x_ref, xn_ref, mod_ref, g_ref, wa_ref, wbt_ref, cw_ref, cb_ref, lg_ref, lb_ref,
                   a_ref, bpt_ref, aext_ref, ash_ref, *, tm):
    i = pl.program_id(1)
    last = pl.num_programs(1) - 1
    g, shift, scale = g_ref[...], mod_ref[0:1, :], mod_ref[1:2, :]

    def glu_rows(xr):
        hr = _norm_mod(xr, g, shift, scale).astype(BF16)
        pa = jnp.dot(hr, wa_ref[...], preferred_element_type=F32)
        return hr, pa[:, :D_A] * jax.nn.sigmoid(pa[:, D_A:])

    h, a_cur = glu_rows(x_ref[...])
    _, a_prev = glu_rows(xp_ref[...])
    _, a_next = glu_rows(xn_ref[...])
    aext_ref[0:CONV_HALO, :] = jnp.where(i > 0, a_prev, 0.0)
    aext_ref[CONV_HALO:CONV_HALO + tm, :] = a_cur
    aext_ref[CONV_HALO + tm:, :] = jnp.where(i < last, a_next, 0.0)

    span = tm + 2 * CONV_HALO - SUBLANES
    for r in range(1, SUBLANES):
        ash_ref[r - 1, 0:span, :] = aext_ref[r:r + span, :]
    acc = jnp.zeros((tm, D_A), F32) + cb_ref[...]
    off = CONV_HALO - CONV_W // 2
    for j in range(CONV_W):
        r, base = (off + j) % SUBLANES, (off + j) // SUBLANES * SUBLANES
        tap = aext_ref[base:base + tm, :] if r == 0 else ash_ref[r - 1, base:base + tm, :]
        acc = acc + cw_ref[j:j + 1, :] * tap
    mu = jnp.mean(acc, axis=-1, keepdims=True)
    cen = acc - mu
    var = jnp.mean(cen * cen, axis=-1, keepdims=True)
    a_ref[...] = _silu(cen * lax.rsqrt(var + EPS) * lg_ref[...] + lb_ref[...]).astype(a_ref.dtype)

    bpt_ref[...] = lax.dot_general(wbt_ref[...], h, (((1,), (1,)), ((), ())), preferred_element_type=F32)


def _mix_in(x, mod3, norm_g, wa, wbt, conv_w, conv_b, ln_g, ln_b, *, tm=512):
    b, l, d = x.shape
    nh = tm // CONV_HALO
    nblk = l // CONV_HALO
    kern = functools.partial(_mix_in_kernel, tm=tm)
    return pl.pallas_call(
        kern,
        grid=(b, l // tm),
        in_specs=[pl.BlockSpec((None, CONV_HALO, d), lambda bi, i: (bi, jnp.maximum(i * nh - 1, 0), 0)),
                  pl.BlockSpec((None, tm, d), lambda bi, i: (bi, i, 0)),
                  pl.BlockSpec((None, CONV_HALO, d), lambda bi, i: (bi, jnp.minimum((i + 1) * nh, nblk - 1), 0)),
                  pl.BlockSpec((None, 3, d), lambda bi, i: (bi, 0, 0)),
                  _const_spec((1, d)),
                  _const_spec(wa.shape),
                  _const_spec(wbt.shape),
                  _const_spec(conv_w.shape),
                  _const_spec((1, D_A)), _const_spec((1, D_A)), _const_spec((1, D_A))],
        out_specs=[pl.BlockSpec((None, tm, D_A), lambda bi, i: (bi, i, 0)),
                   pl.BlockSpec((None, 3 * D_B, tm), lambda bi, i: (bi, 0, i))],
        out_shape=[jax.ShapeDtypeStruct((b, l, D_A), BF16),
                   jax.ShapeDtypeStruct((b, 3 * D_B, l), F32)],
        scratch_shapes=[pltpu.VMEM((tm + 2 * CONV_HALO, D_A), F32),
                        pltpu.VMEM((SUBLANES - 1, tm + 2 * CONV_HALO - SUBLANES, D_A), F32)],
        compiler_params=_params(("arbitrary", "arbitrary")),
        name="mix_in",
    )(x, x, x, mod3, norm_g.reshape(1, d), wa, wbt, conv_w, conv_b.reshape(1, D_A),
      ln_g.reshape(1, D_A), ln_b.reshape(1, D_A))


def _fft_tables(l):
    n = 2 * l
    n1 = n // LANES
    k1 = np.arange(n1)[:, None].astype(np.float64)
    r1 = np.arange(n1)[None, :].astype(np.float64)
    ang1 = 2.0 * np.pi * k1 * r1 / n1
    f1 = np.concatenate([np.cos(ang1), -np.sin(ang1)], axis=0)
    n2 = np.arange(LANES)[None, :].astype(np.float64)
    angt = 2.0 * np.pi * k1 * n2 / n
    tc = np.concatenate([np.cos(angt), np.cos(angt)], axis=1)
    ts = np.concatenate([np.sin(angt), -np.sin(angt)], axis=1)
    a2 = 2.0 * np.pi * np.arange(LANES)[:, None] * np.arange(LANES)[None, :] / LANES
    c2, s2 = np.cos(a2), -np.sin(a2)
    f2f = np.block([[c2, s2], [-s2, c2]])
    f2i = np.block([[c2, -s2], [s2, c2]])
    ang4 = 2.0 * np.pi * np.arange(n1 // 2)[:, None] * np.arange(n1)[None, :] / n1
    g4 = np.concatenate([np.cos(ang4), -np.sin(ang4)], axis=1)
    f32 = lambda a: jnp.asarray(a, F32)
    return dict(f1=f32(f1), f1h=f32(f1[:, :n1 // 2]), tc=f32(tc), ts=f32(ts), f2f=f32(f2f), f2i=f32(f2i),
                g4=f32(g4))


def _swap(a):
    return jnp.concatenate([a[..., LANES:], a[..., :LANES]], axis=-1)


def _stage1_pair(f1, z0, z1, tc, ts):
    zp = jnp.concatenate([z0, z1], axis=1).astype(BF16)
    aa = jnp.dot(f1, zp, preferred_element_type=F32)
    n1 = aa.shape[0] // 2
    outs = []
    for q in range(2):
        a = jnp.concatenate([aa[:n1, q * LANES:(q + 1) * LANES], aa[n1:, q * LANES:(q + 1) * LANES]], axis=1)
        outs.append(a * tc + _swap(a) * ts)
    return outs


def _filt_gen_kernel(w1t_ref, b1_ref, w2t_ref, b2_ref, fr_ref, w3t_ref, dl_ref, k_ref, ssq_ref, *, l, tt):
    i = pl.program_id(0)
    inv_lm1 = 1.0 / (l - 1)
    n = i * tt + lax.broadcasted_iota(jnp.int32, (1, tt), 1)
    j = jnp.where(n < l, n, 2 * l - n).astype(F32)
    t = j * inv_lm1
    w = (2.0 * math.pi) * j / l
    bands = 1e-4 + lax.broadcasted_iota(jnp.int32, (HY_BANDS, 1), 0).astype(F32) * ((HY_BANDS - 1 - 1e-4) / (HY_BANDS - 1))
    fw = bands * w
    fr = fr_ref[...]
    z1 = (w1t_ref[:, 0:1] * t
          + jnp.dot(w1t_ref[:, 1:1 + HY_BANDS], jnp.cos(fw), precision=HI, preferred_element_type=F32)
          - jnp.dot(w1t_ref[:, 1 + HY_BANDS:], jnp.sin(fw), precision=HI, preferred_element_type=F32)
          + b1_ref[...])
    h1 = jnp.sin(fr * z1)
    h2 = jnp.sin(fr * (jnp.dot(w2t_ref[...], h1, precision=HI, preferred_element_type=F32) + b2_ref[...]))
    ht = jnp.dot(w3t_ref[...], h2.astype(BF16), preferred_element_type=F32)
    dec = jnp.exp(-t * dl_ref[...])
    dec = jnp.where(n == l, 0.0, dec)
    k = ht * jnp.concatenate([dec] * HY_ORDER, axis=0)
    k_ref[...] = k

    @pl.when(i == 0)
    def _():
        ssq_ref[...] = jnp.zeros_like(ssq_ref)

    k2 = k * k
    part = k2[:, 0:LANES]
    for q in range(1, tt // LANES):
        part = part + k2[:, q * LANES:(q + 1) * LANES]
    ssq_ref[...] += part


def _filt_gen(hw1, hb1, hw2, hb2, hw3, hfreq, l, *, tt=1024):
    n = 2 * l
    nt = n // tt
    rows = HY_ORDER * D_B
    w3t = hw3.T.reshape(2, rows, HY_HIDDEN).astype(BF16)
    max_decay = math.log(HY_TARGET) / HY_FAST_PCT
    min_decay = math.log(HY_TARGET) / HY_SLOW_PCT
    deltas = jnp.abs(jnp.linspace(min_decay, max_decay, D_B, dtype=F32)).reshape(D_B, 1)
    col = lambda a: a.reshape(HY_HIDDEN, 1)
    kern = functools.partial(_filt_gen_kernel, l=l, tt=tt)
    return pl.pallas_call(
        kern,
        grid=(nt,),
        in_specs=[_const_spec((HY_HIDDEN, 2 * HY_BANDS + 1)), _const_spec((HY_HIDDEN, 1)),
                  _const_spec((HY_HIDDEN, HY_HIDDEN)), _const_spec((HY_HIDDEN, 1)), _const_spec((HY_HIDDEN, 1)),
                  pl.BlockSpec((None, rows, HY_HIDDEN), lambda i: (jnp.where(i * tt >= l, 1, 0), 0, 0)),
                  _const_spec((D_B, 1))],
        out_specs=[pl.BlockSpec((rows, tt), lambda i: (0, i)),
                   pl.BlockSpec((rows, LANES), lambda i: (0, 0))],
        out_shape=[jax.ShapeDtypeStruct((rows, n), F32), jax.ShapeDtypeStruct((rows, LANES), F32)],
        compiler_params=_params(("arbitrary",)),
        name="hyena_filter_gen",
    )(hw1.T, col(hb1), hw2.T, col(hb2), col(hfreq), w3t, deltas)


def _filt_fft_kernel(k_ref, ssq_ref, f1_ref, tc_ref, ts_ref, f2_ref, o_ref, *, ct, n_total):
    tc, ts = tc_ref[...], ts_ref[...]
    f1, f2 = f1_ref[...].astype(BF16), f2_ref[...].astype(BF16)
    n1 = k_ref.shape[1]
    b = []
    for p in range(ct // 2):
        b0, b1 = _stage1_pair(f1, k_ref[2 * p], k_ref[2 * p + 1], tc, ts)
        b.append(jnp.concatenate([b0, b1], axis=0).astype(BF16))
    x = [jnp.dot(bp, f2, preferred_element_type=F32) for bp in b]
    for p in range(ct // 2):
        for q in range(2):
            c = 2 * p + q
            tot = jnp.sum(ssq_ref[c], axis=-1, keepdims=True)
            o_ref[c] = x[p][q * n1:(q + 1) * n1] * (lax.rsqrt(tot + EPS) * (1.0 / n_total))


def _filt_fft(k, ssq, tabs, l, *, ct=8):
    rows = k.shape[0]
    n = 2 * l
    n1 = n // LANES
    kern = functools.partial(_filt_fft_kernel, ct=ct, n_total=n)
    return pl.pallas_call(
        kern,
        grid=(rows // ct,),
        in_specs=[pl.BlockSpec((ct, n1, LANES), lambda c: (c, 0, 0)),
                  pl.BlockSpec((ct, 1, LANES), lambda c: (c, 0, 0)),
                  _const_spec(tabs["f1"].shape), _const_spec(tabs["tc"].shape), _const_spec(tabs["ts"].shape),
                  _const_spec(tabs["f2f"].shape)],
        out_specs=pl.BlockSpec((ct, n1, 2 * LANES), lambda c: (c, 0, 0)),
        out_shape=jax.ShapeDtypeStruct((rows, n1, 2 * LANES), F32),
        compiler_params=_params(("arbitrary",)),
        name="hyena_filter_fft",
    )(k.reshape(rows, n1, LANES), ssq.reshape(rows, 1, LANES), tabs["f1"], tabs["tc"], tabs["ts"], tabs["f2f"])


def _shift_prev(a, row, lane):
    r = pltpu.roll(a, 1, axis=1)
    r2 = pltpu.roll(r, 1, axis=0)
    out = jnp.where(lane == 0, r2, r)
    return jnp.where((lane == 0) & (row == 0), 0.0, out)


def _shift_next(a, row, lane):
    last_l, last_r = a.shape[1] - 1, a.shape[0] - 1
    r = pltpu.roll(a, last_l, axis=1)
    r2 = pltpu.roll(r, last_r, axis=0)
    out = jnp.where(lane == last_l, r2, r)
    return jnp.where((lane == last_l) & (row == last_r), 0.0, out)


def _hyena_kernel(sw_ref, sb_ref, skip_ref, bp_ref, kf_ref, f1_ref, tc_ref, ts_ref, f2f_ref, f2i_ref, g4_ref,
                  o_ref, *, ct):
    ci = pl.program_id(1)
    nh = bp_ref.shape[2]
    n1 = 2 * nh
    npair = ct // 2
    row = lax.broadcasted_iota(jnp.int32, (nh, LANES), 0)
    lane = lax.broadcasted_iota(jnp.int32, (nh, LANES), 1)
    tc, ts = tc_ref[...], ts_ref[...]
    f1, f2f, f2i, g4 = (r[...].astype(BF16) for r in (f1_ref, f2f_ref, f2i_ref, g4_ref))

    def short_conv(part, c):
        ch = part * D_B + ci * ct + c
        a = bp_ref[part, c]
        return (sw_ref[0, ch] * _shift_prev(a, row, lane) + sw_ref[1, ch] * a
                + sw_ref[2, ch] * _shift_next(a, row, lane) + sb_ref[ch])

    def stage1(z0, z1):
        b0, b1 = _stage1_pair(f1, z0, z1, tc, ts)
        return jnp.concatenate([b0, b1], axis=0).astype(BF16)

    def stage2(b):
        return jnp.dot(b, f2f, preferred_element_type=F32)

    def filt(x, order, p):
        outs = []
        for q in range(2):
            kf = kf_ref[order, 2 * p + q]
            kr, ki = kf[:, :LANES], kf[:, LANES:]
            xq = x[q * n1:(q + 1) * n1]
            outs.append(xq * jnp.concatenate([kr, kr], axis=-1) + _swap(xq) * jnp.concatenate([-ki, ki], axis=-1))
        return jnp.concatenate(outs, axis=0).astype(BF16)

    def stage3(y):
        return jnp.dot(y, f2i, preferred_element_type=F32)

    def twiddle_inv(cm):
        d0 = cm[:n1] * tc - _swap(cm[:n1]) * ts
        d1 = cm[n1:] * tc - _swap(cm[n1:]) * ts
        return jnp.concatenate(
            [jnp.concatenate([d0[:, :LANES], d1[:, :LANES]], axis=1),
             jnp.concatenate([d0[:, LANES:], d1[:, LANES:]], axis=1)], axis=0).astype(BF16)

    def stage4(rhs):
        return jnp.dot(g4, rhs, preferred_element_type=F32)

    def long_conv(zs, order):
        b = [stage1(*zs[p]) for p in range(npair)]
        x = [stage2(b[p]) for p in range(npair)]
        y = [filt(x[p], order, p) for p in range(npair)]
        c = [stage3(y[p]) for p in range(npair)]
        d = [twiddle_inv(c[p]) for p in range(npair)]
        o = [stage4(d[p]) for p in range(npair)]
        return [(o[p][:, :LANES], o[p][:, LANES:]) for p in range(npair)]

    v = [(short_conv(0, 2 * p), short_conv(0, 2 * p + 1)) for p in range(npair)]
    y1 = long_conv(v, 0)
    z1 = []
    for p in range(npair):
        pr = []
        for q in range(2):
            c = 2 * p + q
            pr.append(short_conv(1, c) * (y1[p][q] + v[p][q] * skip_ref[0, ci * ct + c]))
        z1.append(tuple(pr))
    y2 = long_conv(z1, 1)
    for p in range(npair):
        for q in range(2):
            c = 2 * p + q
            o_ref[c] = (short_conv(2, c) * (y2[p][q] + z1[p][q] * skip_ref[1, ci * ct + c])).astype(o_ref.dtype)


def _hyena(bpt, kf, short_w, short_b, skip, tabs, *, ct=8):
    b, _, l = bpt.shape
    nh = l // LANES
    n1 = 2 * nh
    smem = pl.BlockSpec(memory_space=pltpu.SMEM)
    kern = functools.partial(_hyena_kernel, ct=ct)
    out = pl.pallas_call(
        kern,
        grid=(b, D_B // ct),
        in_specs=[smem, smem, smem,
                  pl.BlockSpec((None, 3, ct, nh, LANES), lambda bi, c: (bi, 0, c, 0, 0)),
                  pl.BlockSpec((HY_ORDER, ct, n1, 2 * LANES), lambda bi, c: (0, c, 0, 0)),
                  _const_spec(tabs["f1h"].shape), _const_spec(tabs["tc"].shape), _const_spec(tabs["ts"].shape),
                  _const_spec(tabs["f2f"].shape), _const_spec(tabs["f2i"].shape), _const_spec(tabs["g4"].shape)],
        out_specs=pl.BlockSpec((None, ct, nh, LANES), lambda bi, c: (bi, c, 0, 0)),
        out_shape=jax.ShapeDtypeStruct((b, D_B, nh, LANES), BF16),
        compiler_params=_params(("arbitrary", "arbitrary")),
        name="hyena_conv",
    )(short_w, short_b, skip, bpt.reshape(b, 3, D_B, nh, LANES), kf.reshape(HY_ORDER, D_B, n1, 2 * LANES),
      tabs["f1h"], tabs["tc"], tabs["ts"], tabs["f2f"], tabs["f2i"], tabs["g4"])
    return out.reshape(b, D_B, l)


def _mix_out_kernel(x_ref, a_ref, zt_ref, mod_ref, woa_ref, woz_ref, o_ref):
    y = jnp.dot(a_ref[...], woa_ref[...], preferred_element_type=F32)
    y = y + lax.dot_general(zt_ref[...], woz_ref[...], (((0,), (0,)), ((), ())), preferred_element_type=F32)
    o_ref[...] = x_ref[...] + mod_ref[2:3, :] * y


def _mix_out(x, a, zt, mod3, woa, woz, *, tm=512):
    b, l, d = x.shape
    return pl.pallas_call(
        _mix_out_kernel,
        grid=(b, l // tm),
        in_specs=[pl.BlockSpec((None, tm, d), lambda bi, i: (bi, i, 0)),
                  pl.BlockSpec((None, tm, D_A), lambda bi, i: (bi, i, 0)),
                  pl.BlockSpec((None, D_B, tm), lambda bi, i: (bi, 0, i)),
                  pl.BlockSpec((None, 3, d), lambda bi, i: (bi, 0, 0)),
                  _const_spec(woa.shape), _const_spec(woz.shape)],
        out_specs=pl.BlockSpec((None, tm, d), lambda bi, i: (bi, i, 0)),
        out_shape=jax.ShapeDtypeStruct(x.shape, F32),
        compiler_params=_params(("arbitrary", "arbitrary")),
        name="mix_out",
    )(x, a, zt, mod3, woa, woz)


def _qkv_kernel(x_ref, mod_ref, g_ref, w_ref, q_ref, k_ref, v_ref):
    h = _norm_mod(x_ref[...], g_ref[...], mod_ref[0:1, :], mod_ref[1:2, :]).astype(BF16)
    qkv = jnp.dot(h, w_ref[...], preferred_element_type=F32)
    nq, nk = N_HEADS * HEAD_DIM, N_KV * HEAD_DIM
    q_ref[...] = (qkv[:, :nq] * (HEAD_DIM ** -0.5 * LOG2E)).astype(BF16)

    def with_swapped(t):
        sw = [pltpu.roll(t[:, j * LANES:(j + 1) * LANES], HEAD_DIM, axis=1) for j in range(nk // LANES)]
        return jnp.concatenate([t] + sw, axis=1).astype(BF16)

    k_ref[...] = with_swapped(qkv[:, nq:nq + nk])
    v_ref[...] = with_swapped(qkv[:, nq + nk:])


def _qkv(x, mod3, norm_g, w_qkv, *, tm=512):
    b, l, d = x.shape
    nq, nk = N_HEADS * HEAD_DIM, N_KV * HEAD_DIM
    return pl.pallas_call(
        _qkv_kernel,
        grid=(b, l // tm),
        in_specs=[pl.BlockSpec((None, tm, d), lambda bi, i: (bi, i, 0)),
                  pl.BlockSpec((None, 3, d), lambda bi, i: (bi, 0, 0)),
                  _const_spec((1, d)), _const_spec(w_qkv.shape)],
        out_specs=[pl.BlockSpec((None, tm, nq), lambda bi, i: (bi, i, 0)),
                   pl.BlockSpec((None, tm, 2 * nk), lambda bi, i: (bi, i, 0)),
                   pl.BlockSpec((None, tm, 2 * nk), lambda bi, i: (bi, i, 0))],
        out_shape=[jax.ShapeDtypeStruct((b, l, nq), BF16),
                   jax.ShapeDtypeStruct((b, l, 2 * nk), BF16),
                   jax.ShapeDtypeStruct((b, l, 2 * nk), BF16)],
        compiler_params=_params(("arbitrary", "arbitrary")),
        name="attn_qkv",
    )(x, mod3, norm_g.reshape(1, d), w_qkv)


def _alibi_slopes():
    return [2.0 ** (-8.0 * (i + 1) / N_HEADS) for i in range(N_HEADS)]


def _attn_kernel(sink_ref, b0_ref, b1_ref, b2_ref, x_ref, q_ref, kp_ref, kc_ref, kn_ref, vp_ref, vc_ref, vn_ref,
                 mod_ref, wo_ref, o_ref, kall_ref, vall_ref, oh_ref, *, tq, ahead):
    i = pl.program_id(1)
    nk = N_KV * HEAD_DIM
    kall_ref[0:WINDOW, :] = kp_ref[...]
    kall_ref[WINDOW:WINDOW + tq, :] = kc_ref[...]
    kall_ref[WINDOW + tq:, :] = kn_ref[...]
    vall_ref[0:WINDOW, :] = vp_ref[...]
    vall_ref[WINDOW:WINDOW + tq, :] = vc_ref[...]
    vall_ref[WINDOW + tq:, :] = vn_ref[...]

    span = 3 * WINDOW
    lane = lax.broadcasted_iota(jnp.int32, (WINDOW, LANES), 1)
    low = lane < HEAD_DIM

    def sub_block(sb, carry):
        r0 = pl.multiple_of(sb * WINDOW, WINDOW)
        first = (i == 0) & (sb == 0)
        lastb = (i == pl.num_programs(1) - 1) & (sb == tq // WINDOW - 1)

        def cols_for(head):
            kvh = head // GROUP
            tile = kvh // 2
            plain = slice(tile * LANES, (tile + 1) * LANES)
            swapped = slice(nk + tile * LANES, nk + (tile + 1) * LANES)
            return plain if kvh % 2 == head % 2 else swapped

        def scores(head):
            pair = head // 2
            q2 = q_ref[pl.ds(r0, WINDOW), pair * LANES:(pair + 1) * LANES]
            qz = jnp.zeros_like(q2)
            qm = jnp.where(low, q2, qz) if head % 2 == 0 else jnp.where(low, qz, q2)
            kw = kall_ref[pl.ds(r0, span), cols_for(head)]
            s = lax.dot_general(qm, kw, (((1,), (1,)), ((), ())), preferred_element_type=F32)
            bias = jnp.concatenate([b0_ref[jnp.where(first, N_HEADS, head)], b1_ref[head],
                                    b2_ref[jnp.where(lastb, N_HEADS, head)]], axis=1)
            return jnp.maximum(s + bias, NEG)

        pend = [scores(h) for h in range(ahead)]
        outs = []
        for head in range(N_HEADS):
            s = pend.pop(0)
            if head + ahead < N_HEADS:
                pend.append(scores(head + ahead))
            vw = vall_ref[pl.ds(r0, span), cols_for(head)]
            snk = sink_ref[head] * LOG2E
            m = jnp.maximum(jnp.max(s, axis=-1, keepdims=True), snk)
            p = jnp.exp2(s - m)
            den = jnp.sum(p, axis=-1, keepdims=True) + jnp.exp2(snk - m)
            outs.append(jnp.dot(p.astype(BF16), vw, preferred_element_type=F32) / den)
            if head % 2 == 1:
                pair = head // 2
                oh_ref[pl.ds(r0, WINDOW), pair * LANES:(pair + 1) * LANES] = (
                    jnp.where(low, outs[0], outs[1]).astype(BF16))
                outs = []
        return carry

    lax.fori_loop(0, tq // WINDOW, sub_block, 0)
    y = jnp.dot(oh_ref[...], wo_ref[...], preferred_element_type=F32)
    o_ref[...] = x_ref[...] + mod_ref[2:3, :] * y


def _attn(x, q, kk, vv, mod3, sink, w_out, *, tq=512, ahead=3):
    b, l, d = x.shape
    nq, nk2 = q.shape[-1], kk.shape[-1]
    r = tq // WINDOW
    nblk = l // WINDOW
    prev = lambda bi, i: (bi, jnp.maximum(i * r - 1, 0), 0)
    cur = lambda bi, i: (bi, i, 0)
    nxt = lambda bi, i: (bi, jnp.minimum((i + 1) * r, nblk - 1), 0)
    dist = np.abs(np.arange(3 * WINDOW)[None, :] - WINDOW - np.arange(WINDOW)[:, None])
    bias = -np.asarray(_alibi_slopes())[:, None, None] * dist[None] * LOG2E
    bias = np.where(dist[None] <= WINDOW, bias, -np.inf)
    gone = np.full((1, WINDOW, WINDOW), -np.inf)
    b0 = jnp.asarray(np.concatenate([bias[:, :, :WINDOW], gone]), F32)
    b1 = jnp.asarray(bias[:, :, WINDOW:2 * WINDOW], F32)
    b2 = jnp.asarray(np.concatenate([bias[:, :, 2 * WINDOW:], gone]), F32)
    kern = functools.partial(_attn_kernel, tq=tq, ahead=ahead)
    return pl.pallas_call(
        kern,
        grid=(b, l // tq),
        in_specs=[pl.BlockSpec(memory_space=pltpu.SMEM),
                  _const_spec(b0.shape), _const_spec(b1.shape), _const_spec(b2.shape),
                  pl.BlockSpec((None, tq, d), cur),
                  pl.BlockSpec((None, tq, nq), cur),
                  pl.BlockSpec((None, WINDOW, nk2), prev), pl.BlockSpec((None, tq, nk2), cur),
                  pl.BlockSpec((None, WINDOW, nk2), nxt),
                  pl.BlockSpec((None, WINDOW, nk2), prev), pl.BlockSpec((None, tq, nk2), cur),
                  pl.BlockSpec((None, WINDOW, nk2), nxt),
                  pl.BlockSpec((None, 3, d), lambda bi, i: (bi, 0, 0)),
                  _const_spec(w_out.shape)],
        out_specs=pl.BlockSpec((None, tq, d), cur),
        out_shape=jax.ShapeDtypeStruct(x.shape, F32),
        scratch_shapes=[pltpu.VMEM((tq + 2 * WINDOW, nk2), BF16), pltpu.VMEM((tq + 2 * WINDOW, nk2), BF16),
                        pltpu.VMEM((tq, nq), BF16)],
        compiler_params=_params(("arbitrary", "arbitrary")),
        name="attn",
    )(sink, b0, b1, b2, x, q, kk, kk, kk, vv, vv, vv, mod3, w_out)


def _trunk(x, mods, w, filt, tabs):
    m = [mods[0][:, 3 * s:3 * s + 3] for s in range(N_SUB)]
    x = _ffn(x, m[0], w["norm_g"][0, 0], w["ffn_in"][0][0], w["ffn_out"][0][0], w["final_g"], final=False)
    a, bpt = _mix_in(x, m[1], w["norm_g"][0, 1], w["ab_wa"], w["ab_wbt"], w["conv_w"], w["conv_b"],
                     w["conv_ln_g"], w["conv_ln_b"])
    zt = _hyena(bpt, filt, w["hy_short_w"], w["hy_short_b"], w["hy_skip"], tabs)
    x = _mix_out(x, a, zt, m[1], w["ab_woa"], w["ab_woz"])
    x = _ffn(x, m[2], w["norm_g"][0, 2], w["ffn_in"][0][1], w["ffn_out"][0][1], w["final_g"], final=False)

    m = [mods[1][:, 3 * s:3 * s + 3] for s in range(N_SUB)]
    x = _ffn(x, m[0], w["norm_g"][1, 0], w["ffn_in"][1][0], w["ffn_out"][1][0], w["final_g"], final=False)
    q, kk, vv = _qkv(x, m[1], w["norm_g"][1, 1], w["attn_w_qkv"])
    x = _attn(x, q, kk, vv, m[1], w["attn_sink"], w["attn_w_out"])
    x = _ffn(x, m[2], w["norm_g"][1, 2], w["ffn_in"][1][1], w["ffn_out"][1][1], w["final_g"], final=True)
    return x


def kernel(x_prompt, x_sample, c_prompt, c_sample, ada_w, ada_b, norm_g, ffn_w_in, ffn_w_out, final_g, ab_w_in,
           conv_w, conv_b, conv_ln_g, conv_ln_b, hy_short_w, hy_short_b, hy_w1, hy_b1, hy_w2, hy_b2, hy_w3,
           hy_freq, hy_skip, ab_w_out, attn_w_qkv, attn_sink, attn_w_out):
    l = x_prompt.shape[1]
    assert x_sample.shape[1] == l and l % 512 == 0
    bp, bs = x_prompt.shape[0], x_sample.shape[0]
    c_pad = jnp.zeros((8, D_MODEL), F32).at[:bp].set(c_prompt).at[bp:bp + bs].set(c_sample)
    mod = _ada_mod(c_pad, ada_w, ada_b)
    depth = mod.shape[0]
    mod = mod.reshape(depth, 8, 3 * N_SUB, D_MODEL)

    w = dict(
        norm_g=norm_g, final_g=final_g,
        ffn_in=[[ffn_w_in[i, s].astype(BF16) for s in range(2)] for i in range(depth)],
        ffn_out=[[ffn_w_out[i, s].astype(BF16) for s in range(2)] for i in range(depth)],
        ab_wa=ab_w_in[0, :, :2 * D_A].astype(BF16),
        ab_wbt=ab_w_in[0, :, 2 * D_A:].T.astype(BF16),
        conv_w=conv_w[0], conv_b=conv_b[0], conv_ln_g=conv_ln_g[0], conv_ln_b=conv_ln_b[0],
        hy_short_w=hy_short_w[0], hy_short_b=hy_short_b[0], hy_skip=hy_skip[0],
        ab_woa=ab_w_out[0, :D_A].astype(BF16), ab_woz=ab_w_out[0, D_A:].astype(BF16),
        attn_w_qkv=attn_w_qkv[0].astype(BF16), attn_sink=attn_sink[0], attn_w_out=attn_w_out[0].astype(BF16),
    )
    tabs = _fft_tables(l)
    kt, ssq = _filt_gen(hy_w1[0], hy_b1[0], hy_w2[0], hy_b2[0], hy_w3[0], hy_freq[0], l)
    filt = _filt_fft(kt, ssq, tabs, l)

    y_prompt = _trunk(x_prompt, mod[:, :bp], w, filt, tabs)
    y_sample = _trunk(x_sample, mod[:, bp:bp + bs], w, filt, tabs)
    return (y_prompt, y_sample)
```

```python
import functools
import math

import numpy as np
import jax
import jax.numpy as jnp
from jax import lax
from jax.experimental import pallas as pl
from jax.experimental.pallas import tpu as pltpu

F32 = jnp.float32
BF16 = jnp.bfloat16

D_MODEL = 1024
D_FF = 2816
N_SUB = 3
EPS = 1e-6
D_A = 512
CONV_W = 31
CONV_HALO = 16
D_B = 512
HY_ORDER = 2
HY_BANDS = 16
HY_HIDDEN = 64
HY_FAST_PCT = 0.3
HY_SLOW_PCT = 1.5
HY_TARGET = 1e-2
N_HEADS = 16
N_KV = 4
GROUP = 4
HEAD_DIM = 64
WINDOW = 128
NEG = -1e30
LOG2E = math.log2(math.e)

SUBLANES = 8
BF16_ROWS = 16
LANES = 128
VMEM_LIMIT = 56 * 1024 * 1024
HI = lax.Precision.HIGHEST


def _params(sem, vmem=VMEM_LIMIT):
    return pltpu.CompilerParams(dimension_semantics=sem, vmem_limit_bytes=vmem)


def _const_spec(shape):
    nd = len(shape)
    return pl.BlockSpec(shape, lambda *_: (0,) * nd, pipeline_mode=pl.Buffered(1))


def _silu(x):
    return x * jax.nn.sigmoid(x)


def _norm_mod(x, g, shift, scale):
    ms = jnp.mean(x * x, axis=-1, keepdims=True)
    return (x * lax.rsqrt(ms + EPS) * g) * (1.0 + scale) + shift


def _ada_kernel(c_ref, w_ref, b_ref, o_ref):
    cs = _silu(c_ref[...])
    o_ref[...] = jnp.dot(cs, w_ref[...], precision=HI, preferred_element_type=F32) + b_ref[...]


def _ada_mod(c_pad, ada_w, ada_b):
    depth, d, n = ada_w.shape
    tn = 1152
    return pl.pallas_call(
        _ada_kernel,
        grid=(depth, n // tn),
        in_specs=[pl.BlockSpec((8, d), lambda l, j: (0, 0)),
                  pl.BlockSpec((None, d, tn), lambda l, j: (l, 0, j)),
                  pl.BlockSpec((None, 1, tn), lambda l, j: (l, 0, j))],
        out_specs=pl.BlockSpec((None, 8, tn), lambda l, j: (l, 0, j)),
        out_shape=jax.ShapeDtypeStruct((depth, 8, n), F32),
        compiler_params=_params(("arbitrary", "arbitrary")),
        name="ada_mod",
    )(c_pad, ada_w, ada_b.reshape(depth, 1, n))


def _cast_kernel(x_ref, o_ref):
    o_ref[...] = x_ref[...].astype(o_ref.dtype)


def _to_bf16(w, *, block_bytes=8 * 1024 * 1024):
    shape = w.shape
    c = shape[-1]
    w2 = w.reshape(-1, c)
    r = w2.shape[0]
    tr = r
    while tr * c * 4 > block_bytes and tr % (2 * BF16_ROWS) == 0:
        tr //= 2
    out = pl.pallas_call(
        _cast_kernel,
        grid=(r // tr,),
        in_specs=[pl.BlockSpec((tr, c), lambda i: (i, 0))],
        out_specs=pl.BlockSpec((tr, c), lambda i: (i, 0)),
        out_shape=jax.ShapeDtypeStruct((r, c), BF16),
        compiler_params=_params(("arbitrary",)),
        name="cast_bf16",
    )(w2)
    return out.reshape(shape)


def _ffn_kernel(x_ref, mod_ref, g_ref, win_ref, wout_ref, fg_ref, o_ref, *, final, halves):
    rows = x_ref.shape[0] // halves
    for hf in range(halves):
        rs = slice(hf * rows, (hf + 1) * rows)
        x = x_ref[rs, :]
        h = _norm_mod(x, g_ref[...], mod_ref[0:1, :], mod_ref[1:2, :]).astype(BF16)
        gate = jnp.dot(h, win_ref[:, :D_FF], preferred_element_type=F32)
        up = jnp.dot(h, win_ref[:, D_FF:], preferred_element_type=F32)
        act = (_silu(gate) * up).astype(BF16)
        y = x + (0.5 * mod_ref[2:3, :]) * jnp.dot(act, wout_ref[...], preferred_element_type=F32)
        if final:
            y = y * lax.rsqrt(jnp.mean(y * y, axis=-1, keepdims=True) + EPS) * fg_ref[...]
        o_ref[rs, :] = y


def _ffn(x, mod3, norm_g, w_in, w_out, final_g, *, final, tm=1024, halves=2):
    b, l, d = x.shape
    kern = functools.partial(_ffn_kernel, final=final, halves=halves)
    return pl.pallas_call(
        kern,
        grid=(b, l // tm),
        in_specs=[pl.BlockSpec((None, tm, d), lambda bi, i: (bi, i, 0)),
                  pl.BlockSpec((None, 3, d), lambda bi, i: (bi, 0, 0)),
                  _const_spec((1, d)),
                  _const_spec(w_in.shape),
                  _const_spec(w_out.shape),
                  _const_spec((1, d))],
        out_specs=pl.BlockSpec((None, tm, d), lambda bi, i: (bi, i, 0)),
        out_shape=jax.ShapeDtypeStruct(x.shape, F32),
        compiler_params=_params(("arbitrary", "arbitrary")),
        name="ffn",
    )(x, mod3, norm_g.reshape(1, d), w_in, w_out, final_g.reshape(1, d))


def _conv_norm_swish(aext_ref, ash_ref, cw_ref, cb, lg, lb, r0, rows):
    span = rows + 2 * CONV_HALO - SUBLANES
    for r in range(1, SUBLANES):
        ash_ref[r - 1, 0:span, :] = aext_ref[r0 + r:r0 + r + span, :]
    acc = jnp.zeros((rows, D_A), F32) + cb
    off = CONV_HALO - CONV_W // 2
    for j in range(CONV_W):
        r, base = (off + j) % SUBLANES, (off + j) // SUBLANES * SUBLANES
        tap = aext_ref[r0 + base:r0 + base + rows, :] if r == 0 else ash_ref[r - 1, base:base + rows, :]
        acc = acc + cw_ref[j:j + 1, :] * tap
    mu = jnp.mean(acc, axis=-1, keepdims=True)
    cen = acc - mu
    var = jnp.mean(cen * cen, axis=-1, keepdims=True)
    return _silu(cen * lax.rsqrt(var + EPS) * lg + lb)


def _ffn_fused_kernel(*refs, pre, post, halves):
    refs = list(refs)
    take = lambda n: [refs.pop(0) for _ in range(n)]
    x_ref, mod_ref, g_ref, win_ref, wout_ref = take(5)
    if pre == "mix":
        ap_ref, ac_ref, an_ref, zt_ref, gmix_ref, woa_ref, woz_ref, cw_ref, cb_ref, lg_ref, lb_ref = take(11)
    if post == "proj":
        pmod_ref, pg_ref, wa_ref, wbt_ref = take(4)
        o_ref, a_ref, bpt_ref = take(3)
    elif post == "qkv":
        pmod_ref, pg_ref, wqkv_ref = take(3)
        o_ref, q_ref, k_ref, v_ref = take(4)
    else:
        (o_ref,) = take(1)
    if pre == "mix":
        aext_ref, ash_ref = take(2)
    assert not refs

    tm = x_ref.shape[0]
    rows = tm // halves
    i = pl.program_id(1)
    if pre == "mix":
        aext_ref[0:CONV_HALO, :] = jnp.where(i > 0, ap_ref[...], 0.0)
        aext_ref[CONV_HALO:CONV_HALO + tm, :] = ac_ref[...]
        aext_ref[CONV_HALO + tm:, :] = jnp.where(i < pl.num_programs(1) - 1, an_ref[...], 0.0)

    def start(hf):
        rs = slice(hf * rows, (hf + 1) * rows)
        x = x_ref[rs, :]
        if pre == "mix":
            a = _conv_norm_swish(aext_ref, ash_ref, cw_ref, cb_ref[...], lg_ref[...], lb_ref[...], hf * rows, rows)
            y = jnp.dot(a.astype(BF16), woa_ref[...], preferred_element_type=F32)
            y = y + lax.dot_general(zt_ref[:, rs], woz_ref[...], (((0,), (0,)), ((), ())),
                                    preferred_element_type=F32)
            x = x + gmix_ref[...] * y
        h = _norm_mod(x, g_ref[...], mod_ref[0:1, :], mod_ref[1:2, :]).astype(BF16)
        gate = jnp.dot(h, win_ref[:, :D_FF], preferred_element_type=F32)
        up = jnp.dot(h, win_ref[:, D_FF:], preferred_element_type=F32)
        return rs, x, gate, up

    def finish(st):
        rs, x, gate, up = st
        act = (_silu(gate) * up).astype(BF16)
        y = x + (0.5 * mod_ref[2:3, :]) * jnp.dot(act, wout_ref[...], preferred_element_type=F32)
        o_ref[rs, :] = y
        return rs, y

    def epilogue(done):
        rs, y = done
        if post is None:
            return
        h2 = _norm_mod(y, pg_ref[...], pmod_ref[0:1, :], pmod_ref[1:2, :]).astype(BF16)
        if post == "proj":
            pa = jnp.dot(h2, wa_ref[...], preferred_element_type=F32)
            a_ref[rs, :] = pa[:, :D_A] * jax.nn.sigmoid(pa[:, D_A:])
            bpt_ref[:, rs] = lax.dot_general(wbt_ref[...], h2, (((1,), (1,)), ((), ())),
                                             preferred_element_type=F32)
        else:
            qkv = jnp.dot(h2, wqkv_ref[...], preferred_element_type=F32)
            nq, nk = N_HEADS * HEAD_DIM, N_KV * HEAD_DIM
            q_ref[rs, :] = (qkv[:, :nq] * (HEAD_DIM ** -0.5 * LOG2E)).astype(BF16)

            def with_swapped(t):
                sw = [pltpu.roll(t[:, j * LANES:(j + 1) * LANES], HEAD_DIM, axis=1) for j in range(nk // LANES)]
                return jnp.concatenate([t] + sw, axis=1).astype(BF16)

            k_ref[rs, :] = with_swapped(qkv[:, nq:nq + nk])
            v_ref[rs, :] = with_swapped(qkv[:, nq + nk:])

    st = start(0)
    for hf in range(halves):
        done = finish(st)
        if hf + 1 < halves:
            st = start(hf + 1)
        epilogue(done)


def _ffn_fused(x, mod3, norm_g, w_in, w_out, *, pre=None, post=None, tm=512, halves=2):
    b, l, d = x.shape
    tile = lambda w: pl.BlockSpec((None, tm, w), lambda bi, i: (bi, i, 0))
    per_seq = lambda r: pl.BlockSpec((None, r, d), lambda bi, i: (bi, 0, 0))
    args = [x, mod3, norm_g.reshape(1, d), w_in, w_out]
    specs = [tile(d), per_seq(3), _const_spec((1, d)), _const_spec(w_in.shape), _const_spec(w_out.shape)]
    scratch = []
    pre_kind = post_kind = None
    if pre is not None:
        pre_kind, a_glu, zt, gate_mix, woa, woz, conv_w, conv_b, ln_g, ln_b = pre
        nh, nblk = tm // CONV_HALO, l // CONV_HALO
        args += [a_glu, a_glu, a_glu, zt, gate_mix, woa, woz, conv_w, conv_b.reshape(1, D_A), ln_g.reshape(1, D_A),
                 ln_b.reshape(1, D_A)]
        specs += [pl.BlockSpec((None, CONV_HALO, D_A), lambda bi, i: (bi, jnp.maximum(i * nh - 1, 0), 0)),
                  tile(D_A),
                  pl.BlockSpec((None, CONV_HALO, D_A), lambda bi, i: (bi, jnp.minimum((i + 1) * nh, nblk - 1), 0)),
                  pl.BlockSpec((None, D_B, tm), lambda bi, i: (bi, 0, i)),
                  per_seq(1), _const_spec(woa.shape), _const_spec(woz.shape), _const_spec(conv_w.shape),
                  _const_spec((1, D_A)), _const_spec((1, D_A)), _const_spec((1, D_A))]
        rows = tm // halves
        scratch = [pltpu.VMEM((tm + 2 * CONV_HALO, D_A), F32),
                   pltpu.VMEM((SUBLANES - 1, rows + 2 * CONV_HALO - SUBLANES, D_A), F32)]
    out_specs = [tile(d)]
    out_shape = [jax.ShapeDtypeStruct(x.shape, F32)]
    if post is not None:
        post_kind = post[0]
        args += [post[1], post[2].reshape(1, d)] + list(post[3:])
        specs += [per_seq(3), _const_spec((1, d))] + [_const_spec(w.shape) for w in post[3:]]
        if post_kind == "proj":
            out_specs += [tile(D_A), pl.BlockSpec((None, 3 * D_B, tm), lambda bi, i: (bi, 0, i))]
            out_shape += [jax.ShapeDtypeStruct((b, l, D_A), F32), jax.ShapeDtypeStruct((b, 3 * D_B, l), F32)]
        else:
            nq, nk = N_HEADS * HEAD_DIM, N_KV * HEAD_DIM
            out_specs += [tile(nq), tile(2 * nk), tile(2 * nk)]
            out_shape += [jax.ShapeDtypeStruct((b, l, nq), BF16), jax.ShapeDtypeStruct((b, l, 2 * nk), BF16),
                          jax.ShapeDtypeStruct((b, l, 2 * nk), BF16)]
    kern = functools.partial(_ffn_fused_kernel, pre=pre_kind, post=post_kind, halves=halves)
    return pl.pallas_call(
        kern,
        grid=(b, l // tm),
        in_specs=specs,
        out_specs=out_specs,
        out_shape=out_shape,
        scratch_shapes=scratch,
        compiler_params=_params(("arbitrary", "arbitrary")),
        name="ffn_" + "_".join(k for k in (pre_kind, post_kind) if k),
    )(*args)


def _fft_tables(l):
    n = 2 * l
    n1 = n // LANES
    h = n1 // 2
    nr = min(n1, -(-(h + 1) // BF16_ROWS) * BF16_ROWS)
    k1 = np.arange(nr)[:, None].astype(np.float64)
    r1 = np.arange(n1)[None, :].astype(np.float64)
    ang1 = 2.0 * np.pi * k1 * r1 / n1
    f1 = np.concatenate([np.cos(ang1), -np.sin(ang1)], axis=0)
    n2 = np.arange(LANES)[None, :].astype(np.float64)
    angt = 2.0 * np.pi * k1 * n2 / n
    tc = np.concatenate([np.cos(angt), np.cos(angt)], axis=1)
    ts = np.concatenate([np.sin(angt), -np.sin(angt)], axis=1)
    a2 = 2.0 * np.pi * np.arange(LANES)[:, None] * np.arange(LANES)[None, :] / LANES
    c2, s2 = np.cos(a2), -np.sin(a2)
    f2f = np.block([[c2, s2], [-s2, c2]])
    f2i = np.block([[c2, -s2], [s2, c2]])
    ang4 = 2.0 * np.pi * np.arange(h)[:, None] * np.arange(h)[None, :] / n1
    wgt = np.where(np.arange(h)[None, :] == 0, 1.0, 2.0)
    g4 = np.concatenate([wgt * np.cos(ang4), -wgt * np.sin(ang4)], axis=1)
    sgn = np.broadcast_to(((-1.0) ** np.arange(h))[:, None], (h, 2 * LANES))
    f32 = lambda a: jnp.asarray(a, F32)
    return dict(f1=f32(f1), f1h=f32(f1[:, :n1 // 2]), tc=f32(tc), ts=f32(ts), f2f=f32(f2f), f2i=f32(f2i),
                g4=f32(g4), sgn=f32(sgn))


def _swap(a):
    return jnp.concatenate([a[..., LANES:], a[..., :LANES]], axis=-1)


def _stage1_pair(f1, z0, z1, tc, ts):
    zp = jnp.concatenate([z0, z1], axis=1).astype(BF16)
    aa = jnp.dot(f1, zp, preferred_element_type=F32)
    nr = aa.shape[0] // 2
    outs = []
    for q in range(2):
        a = jnp.concatenate([aa[:nr, q * LANES:(q + 1) * LANES], aa[nr:, q * LANES:(q + 1) * LANES]], axis=1)
        a = a.astype(BF16)
        outs.append(a * tc + _swap(a) * ts)
    return outs


def _filt_gen_kernel(w1t_ref, b1_ref, w2t_ref, b2_ref, fr_ref, w3t_ref, dl_ref, k_ref, ssq_ref, *, l, tt):
    i = pl.program_id(0)
    inv_lm1 = 1.0 / (l - 1)
    n = i * tt + lax.broadcasted_iota(jnp.int32, (1, tt), 1)
    j = jnp.where(n < l, n, 2 * l - n).astype(F32)
    t = j * inv_lm1
    w = (2.0 * math.pi) * j / l
    bands = 1e-4 + lax.broadcasted_iota(jnp.int32, (HY_BANDS, 1), 0).astype(F32) * ((HY_BANDS - 1 - 1e-4) / (HY_BANDS - 1))
    fw = bands * w
    fr = fr_ref[...]
    z1 = (w1t_ref[:, 0:1] * t
          + jnp.dot(w1t_ref[:, 1:1 + HY_BANDS], jnp.cos(fw), precision=HI, preferred_element_type=F32)
          - jnp.dot(w1t_ref[:, 1 + HY_BANDS:], jnp.sin(fw), precision=HI, preferred_element_type=F32)
          + b1_ref[...])
    h1 = jnp.sin(fr * z1)
    h2 = jnp.sin(fr * (jnp.dot(w2t_ref[...], h1, precision=HI, preferred_element_type=F32) + b2_ref[...]))
    ht = jnp.dot(w3t_ref[...], h2.astype(BF16), preferred_element_type=F32)
    dec = jnp.exp(-t * dl_ref[...])
    dec = jnp.where(n == l, 0.0, dec)
    k = ht * jnp.concatenate([dec] * HY_ORDER, axis=0)
    k_ref[...] = k

    @pl.when(i == 0)
    def _():
        ssq_ref[...] = jnp.zeros_like(ssq_ref)

    k2 = k * k
    part = k2[:, 0:LANES]
    for q in range(1, tt // LANES):
        part = part + k2[:, q * LANES:(q + 1) * LANES]
    ssq_ref[...] += part


def _filt_gen(hw1, hb1, hw2, hb2, hw3, hfreq, l, *, tt=1024):
    n = 2 * l
    nt = n // tt
    rows = HY_ORDER * D_B
    w3t = hw3.T.reshape(2, rows, HY_HIDDEN).astype(BF16)
    max_decay = math.log(HY_TARGET) / HY_FAST_PCT
    min_decay = math.log(HY_TARGET) / HY_SLOW_PCT
    deltas = jnp.abs(jnp.linspace(min_decay, max_decay, D_B, dtype=F32)).reshape(D_B, 1)
    col = lambda a: a.reshape(HY_HIDDEN, 1)
    kern = functools.partial(_filt_gen_kernel, l=l, tt=tt)
    return pl.pallas_call(
        kern,
        grid=(nt,),
        in_specs=[_const_spec((HY_HIDDEN, 2 * HY_BANDS + 1)), _const_spec((HY_HIDDEN, 1)),
                  _const_spec((HY_HIDDEN, HY_HIDDEN)), _const_spec((HY_HIDDEN, 1)), _const_spec((HY_HIDDEN, 1)),
                  pl.BlockSpec((None, rows, HY_HIDDEN), lambda i: (jnp.where(i * tt >= l, 1, 0), 0, 0)),
                  _const_spec((D_B, 1))],
        out_specs=[pl.BlockSpec((rows, tt), lambda i: (0, i)),
                   pl.BlockSpec((rows, LANES), lambda i: (0, 0))],
        out_shape=[jax.ShapeDtypeStruct((rows, n), F32), jax.ShapeDtypeStruct((rows, LANES), F32)],
        compiler_params=_params(("arbitrary",)),
        name="hyena_filter_gen",
    )(hw1.T, col(hb1), hw2.T, col(hb2), col(hfreq), w3t, deltas)


def _filt_fft_kernel(k_ref, ssq_ref, f1_ref, tc_ref, ts_ref, f2_ref, o_ref, *, ct, n_total):
    tc, ts = tc_ref[...].astype(BF16), ts_ref[...].astype(BF16)
    f1, f2 = f1_ref[...].astype(BF16), f2_ref[...].astype(BF16)
    nr = tc.shape[0]
    b = []
    for p in range(ct // 2):
        b0, b1 = _stage1_pair(f1, k_ref[2 * p], k_ref[2 * p + 1], tc, ts)
        b.append(jnp.concatenate([b0, b1], axis=0))
    x = [jnp.dot(bp, f2, preferred_element_type=F32) for bp in b]
    for p in range(ct // 2):
        for q in range(2):
            c = 2 * p + q
            tot = jnp.sum(ssq_ref[c], axis=-1, keepdims=True)
            o_ref[c] = (x[p][q * nr:(q + 1) * nr] * (lax.rsqrt(tot + EPS) * (1.0 / n_total))).astype(o_ref.dtype)


def _filt_fft(k, ssq, tabs, l, *, ct=8):
    rows = k.shape[0]
    n = 2 * l
    n1 = n // LANES
    nr = tabs["tc"].shape[0]
    kern = functools.partial(_filt_fft_kernel, ct=ct, n_total=n)
    return pl.pallas_call(
        kern,
        grid=(rows // ct,),
        in_specs=[pl.BlockSpec((ct, n1, LANES), lambda c: (c, 0, 0)),
                  pl.BlockSpec((ct, 1, LANES), lambda c: (c, 0, 0)),
                  _const_spec(tabs["f1"].shape), _const_spec(tabs["tc"].shape), _const_spec(tabs["ts"].shape),
                  _const_spec(tabs["f2f"].shape)],
        out_specs=pl.BlockSpec((ct, nr, 2 * LANES), lambda c: (c, 0, 0)),
        out_shape=jax.ShapeDtypeStruct((rows, nr, 2 * LANES), BF16),
        compiler_params=_params(("arbitrary",)),
        name="hyena_filter_fft",
    )(k.reshape(rows, n1, LANES), ssq.reshape(rows, 1, LANES), tabs["f1"], tabs["tc"], tabs["ts"], tabs["f2f"])


def _shift_prev(a, row, lane):
    nv = a.shape[0] // SUBLANES
    r3 = pltpu.roll(a, 1, axis=1).reshape(nv, SUBLANES, LANES)
    rot = pltpu.roll(r3, 1, axis=1)
    rot_prev = jnp.concatenate([jnp.zeros((1, SUBLANES, LANES), a.dtype), rot[:-1]], axis=0)
    out = jnp.where((lane == 0) & (row == 0), rot_prev, jnp.where(lane == 0, rot, r3))
    return out.reshape(a.shape)


def _shift_next(a, row, lane):
    nv = a.shape[0] // SUBLANES
    r3 = pltpu.roll(a, LANES - 1, axis=1).reshape(nv, SUBLANES, LANES)
    rot = pltpu.roll(r3, SUBLANES - 1, axis=1)
    rot_next = jnp.concatenate([rot[1:], jnp.zeros((1, SUBLANES, LANES), a.dtype)], axis=0)
    out = jnp.where((lane == LANES - 1) & (row == SUBLANES - 1), rot_next, jnp.where(lane == LANES - 1, rot, r3))
    return out.reshape(a.shape)


def _hyena_kernel(sw_ref, sb_ref, skip_ref, bp_ref, kf_ref, f1_ref, tc_ref, ts_ref, f2f_ref, f2i_ref, g4_ref,
                  sgn_ref, o_ref, *, ct):
    ci = pl.program_id(1)
    nh = bp_ref.shape[2]
    nr = tc_ref.shape[0]
    npair = ct // 2
    row = lax.broadcasted_iota(jnp.int32, (nh // SUBLANES, SUBLANES, LANES), 1)
    lane = lax.broadcasted_iota(jnp.int32, (nh // SUBLANES, SUBLANES, LANES), 2)
    tc, ts = tc_ref[...].astype(BF16), ts_ref[...].astype(BF16)
    f1, f2f, f2i, g4 = (r[...].astype(BF16) for r in (f1_ref, f2f_ref, f2i_ref, g4_ref))

    def short_conv(part, c):
        ch = part * D_B + ci * ct + c
        a = bp_ref[part, c]
        return (sw_ref[0, ch] * _shift_prev(a, row, lane) + sw_ref[1, ch] * a
                + sw_ref[2, ch] * _shift_next(a, row, lane) + sb_ref[ch])

    def stage1(z0, z1):
        b0, b1 = _stage1_pair(f1, z0, z1, tc, ts)
        return jnp.concatenate([b0, b1], axis=0)

    def stage2(b):
        return jnp.dot(b, f2f, preferred_element_type=F32)

    def filt(x, order, p):
        outs = []
        for q in range(2):
            kf = kf_ref[order, 2 * p + q]
            kr, ki = kf[:, :LANES], kf[:, LANES:]
            xq = x[q * nr:(q + 1) * nr].astype(BF16)
            outs.append(xq * jnp.concatenate([kr, kr], axis=-1) + _swap(xq) * jnp.concatenate([-ki, ki], axis=-1))
        return jnp.concatenate(outs, axis=0)

    def stage3(y):
        return jnp.dot(y, f2i, preferred_element_type=F32)

    def twiddle_inv(cm):
        cm = cm.astype(BF16)
        d0 = cm[:nr] * tc - _swap(cm[:nr]) * ts
        d1 = cm[nr:] * tc - _swap(cm[nr:]) * ts
        rhs = jnp.concatenate(
            [jnp.concatenate([d0[:nh, :LANES], d1[:nh, :LANES]], axis=1),
             jnp.concatenate([d0[:nh, LANES:], d1[:nh, LANES:]], axis=1)], axis=0)
        mid = jnp.concatenate([d0[nh:nh + 1, :LANES], d1[nh:nh + 1, :LANES]], axis=1).astype(F32)
        return rhs, mid

    def stage4(d):
        rhs, mid = d
        return jnp.dot(g4, rhs, preferred_element_type=F32) + sgn_ref[...] * mid

    def long_conv(zs, order):
        b = [stage1(*zs[p]) for p in range(npair)]
        x = [stage2(b[p]) for p in range(npair)]
        y = [filt(x[p], order, p) for p in range(npair)]
        c = [stage3(y[p]) for p in range(npair)]
        d = [twiddle_inv(c[p]) for p in range(npair)]
        o = [stage4(d[p]) for p in range(npair)]
        return [(o[p][:, :LANES], o[p][:, LANES:]) for p in range(npair)]

    v = [(short_conv(0, 2 * p), short_conv(0, 2 * p + 1)) for p in range(npair)]
    y1 = long_conv(v, 0)
    z1 = []
    for p in range(npair):
        pr = []
        for q in range(2):
            c = 2 * p + q
            pr.append(short_conv(1, c) * (y1[p][q] + v[p][q] * skip_ref[0, ci * ct + c]))
        z1.append(tuple(pr))
    y2 = long_conv(z1, 1)
    for p in range(npair):
        for q in range(2):
            c = 2 * p + q
            o_ref[c] = (short_conv(2, c) * (y2[p][q] + z1[p][q] * skip_ref[1, ci * ct + c])).astype(o_ref.dtype)


def _hyena(bpt, kf, short_w, short_b, skip, tabs, *, ct=8):
    b, _, l = bpt.shape
    nh = l // LANES
    nr = tabs["tc"].shape[0]
    smem = pl.BlockSpec(memory_space=pltpu.SMEM)
    kern = functools.partial(_hyena_kernel, ct=ct)
    out = pl.pallas_call(
        kern,
        grid=(b, D_B // ct),
        in_specs=[smem, smem, smem,
                  pl.BlockSpec((None, 3, ct, nh, LANES), lambda bi, c: (bi, 0, c, 0, 0)),
                  pl.BlockSpec((HY_ORDER, ct, nr, 2 * LANES), lambda bi, c: (0, c, 0, 0)),
                  _const_spec(tabs["f1h"].shape), _const_spec(tabs["tc"].shape), _const_spec(tabs["ts"].shape),
                  _const_spec(tabs["f2f"].shape), _const_spec(tabs["f2i"].shape), _const_spec(tabs["g4"].shape),
                  _const_spec(tabs["sgn"].shape)],
        out_specs=pl.BlockSpec((None, ct, nh, LANES), lambda bi, c: (bi, c, 0, 0)),
        out_shape=jax.ShapeDtypeStruct((b, D_B, nh, LANES), BF16),
        compiler_params=_params(("arbitrary", "arbitrary")),
        name="hyena_conv",
    )(short_w, short_b, skip, bpt.reshape(b, 3, D_B, nh, LANES), kf.reshape(HY_ORDER, D_B, nr, 2 * LANES),
      tabs["f1h"], tabs["tc"], tabs["ts"], tabs["f2f"], tabs["f2i"], tabs["g4"], tabs["sgn"])
    return out.reshape(b, D_B, l)


def _alibi_slopes():
    return [2.0 ** (-8.0 * (i + 1) / N_HEADS) for i in range(N_HEADS)]


def _attn_kernel(sink_ref, b0_ref, b1_ref, b2_ref, x_ref, q_ref, kp_ref, kc_ref, kn_ref, vp_ref, vc_ref, vn_ref,
                 mod_ref, wo_ref, o_ref, kall_ref, vall_ref, oh_ref, *, tq, ahead):
    i = pl.program_id(1)
    nk = N_KV * HEAD_DIM
    kall_ref[0:WINDOW, :] = kp_ref[...]
    kall_ref[WINDOW:WINDOW + tq, :] = kc_ref[...]
    kall_ref[WINDOW + tq:, :] = kn_ref[...]
    vall_ref[0:WINDOW, :] = vp_ref[...]
    vall_ref[WINDOW:WINDOW + tq, :] = vc_ref[...]
    vall_ref[WINDOW + tq:, :] = vn_ref[...]

    span = 3 * WINDOW
    lane = lax.broadcasted_iota(jnp.int32, (WINDOW, LANES), 1)
    low = lane < HEAD_DIM

    def sub_block(sb, carry):
        r0 = pl.multiple_of(sb * WINDOW, WINDOW)
        first = (i == 0) & (sb == 0)
        lastb = (i == pl.num_programs(1) - 1) & (sb == tq // WINDOW - 1)

        def cols_for(head):
            kvh = head // GROUP
            tile = kvh // 2
            plain = slice(tile * LANES, (tile + 1) * LANES)
            swapped = slice(nk + tile * LANES, nk + (tile + 1) * LANES)
            return plain if kvh % 2 == head % 2 else swapped

        def scores(head):
            pair = head // 2
            q2 = q_ref[pl.ds(r0, WINDOW), pair * LANES:(pair + 1) * LANES]
            qz = jnp.zeros_like(q2)
            qm = jnp.where(low, q2, qz) if head % 2 == 0 else jnp.where(low, qz, q2)
            kw = kall_ref[pl.ds(r0, span), cols_for(head)]
            s = lax.dot_general(qm, kw, (((1,), (1,)), ((), ())), preferred_element_type=F32)
            bias = jnp.concatenate([b0_ref[jnp.where(first, N_HEADS, head)], b1_ref[head],
                                    b2_ref[jnp.where(lastb, N_HEADS, head)]], axis=1)
            return jnp.maximum(s + bias, NEG)

        pend = [scores(h) for h in range(ahead)]
        outs = []
        for head in range(N_HEADS):
            s = pend.pop(0)
            if head + ahead < N_HEADS:
                pend.append(scores(head + ahead))
            vw = vall_ref[pl.ds(r0, span), cols_for(head)]
            snk = sink_ref[head] * LOG2E
            m = jnp.maximum(jnp.max(s, axis=-1, keepdims=True), snk)
            p = jnp.exp2(s - m)
            den = jnp.sum(p, axis=-1, keepdims=True) + jnp.exp2(snk - m)
            outs.append(jnp.dot(p.astype(BF16), vw, preferred_element_type=F32) / den)
            if head % 2 == 1:
                pair = head // 2
                oh_ref[pl.ds(r0, WINDOW), pair * LANES:(pair + 1) * LANES] = (
                    jnp.where(low, outs[0], outs[1]).astype(BF16))
                outs = []
        return carry

    lax.fori_loop(0, tq // WINDOW, sub_block, 0)
    y = jnp.dot(oh_ref[...], wo_ref[...], preferred_element_type=F32)
    o_ref[...] = x_ref[...] + mod_ref[2:3, :] * y


def _attn(x, q, kk, vv, mod3, sink, w_out, *, tq=512, ahead=3):
    b, l, d = x.shape
    nq, nk2 = q.shape[-1], kk.shape[-1]
    r = tq // WINDOW
    nblk = l // WINDOW
    prev = lambda bi, i: (bi, jnp.maximum(i * r - 1, 0), 0)
    cur = lambda bi, i: (bi, i, 0)
    nxt = lambda bi, i: (bi, jnp.minimum((i + 1) * r, nblk - 1), 0)
    dist = np.abs(np.arange(3 * WINDOW)[None, :] - WINDOW - np.arange(WINDOW)[:, None])
    bias = -np.asarray(_alibi_slopes())[:, None, None] * dist[None] * LOG2E
    bias = np.where(dist[None] <= WINDOW, bias, -np.inf)
    gone = np.full((1, WINDOW, WINDOW), -np.inf)
    b0 = jnp.asarray(np.concatenate([bias[:, :, :WINDOW], gone]), F32)
    b1 = jnp.asarray(bias[:, :, WINDOW:2 * WINDOW], F32)
    b2 = jnp.asarray(np.concatenate([bias[:, :, 2 * WINDOW:], gone]), F32)
    kern = functools.partial(_attn_kernel, tq=tq, ahead=ahead)
    return pl.pallas_call(
        kern,
        grid=(b, l // tq),
        in_specs=[pl.BlockSpec(memory_space=pltpu.SMEM),
                  _const_spec(b0.shape), _const_spec(b1.shape), _const_spec(b2.shape),
                  pl.BlockSpec((None, tq, d), cur),
                  pl.BlockSpec((None, tq, nq), cur),
                  pl.BlockSpec((None, WINDOW, nk2), prev), pl.BlockSpec((None, tq, nk2), cur),
                  pl.BlockSpec((None, WINDOW, nk2), nxt),
                  pl.BlockSpec((None, WINDOW, nk2), prev), pl.BlockSpec((None, tq, nk2), cur),
                  pl.BlockSpec((None, WINDOW, nk2), nxt),
                  pl.BlockSpec((None, 3, d), lambda bi, i: (bi, 0, 0)),
                  _const_spec(w_out.shape)],
        out_specs=pl.BlockSpec((None, tq, d), cur),
        out_shape=jax.ShapeDtypeStruct(x.shape, F32),
        scratch_shapes=[pltpu.VMEM((tq + 2 * WINDOW, nk2), BF16), pltpu.VMEM((tq + 2 * WINDOW, nk2), BF16),
                        pltpu.VMEM((tq, nq), BF16)],
        compiler_params=_params(("arbitrary", "arbitrary")),
        name="attn",
    )(sink, b0, b1, b2, x, q, kk, kk, kk, vv, vv, vv, mod3, w_out)


def _trunk(x, mods, w, filt, tabs):
    m = [mods[0][:, 3 * s:3 * s + 3] for s in range(N_SUB)]
    x, a_glu, bpt = _ffn_fused(x, m[0], w["norm_g"][0, 0], w["ffn_in"][0][0], w["ffn_out"][0][0],
                               post=("proj", m[1], w["norm_g"][0, 1], w["ab_wa"], w["ab_wbt"]))
    zt = _hyena(bpt, filt, w["hy_short_w"], w["hy_short_b"], w["hy_skip"], tabs)
    x = _ffn_fused(x, m[2], w["norm_g"][0, 2], w["ffn_in"][0][1], w["ffn_out"][0][1],
                   pre=("mix", a_glu, zt, m[1][:, 2:3], w["ab_woa"], w["ab_woz"], w["conv_w"], w["conv_b"],
                        w["conv_ln_g"], w["conv_ln_b"]))[0]

    m = [mods[1][:, 3 * s:3 * s + 3] for s in range(N_SUB)]
    x, q, kk, vv = _ffn_fused(x, m[0], w["norm_g"][1, 0], w["ffn_in"][1][0], w["ffn_out"][1][0],
                              post=("qkv", m[1], w["norm_g"][1, 1], w["attn_w_qkv"]))
    x = _attn(x, q, kk, vv, m[1], w["attn_sink"], w["attn_w_out"])
    x = _ffn(x, m[2], w["norm_g"][1, 2], w["ffn_in"][1][1], w["ffn_out"][1][1], w["final_g"], final=True)
    return x


def kernel(x_prompt, x_sample, c_prompt, c_sample, ada_w, ada_b, norm_g, ffn_w_in, ffn_w_out, final_g, ab_w_in,
           conv_w, conv_b, conv_ln_g, conv_ln_b, hy_short_w, hy_short_b, hy_w1, hy_b1, hy_w2, hy_b2, hy_w3,
           hy_freq, hy_skip, ab_w_out, attn_w_qkv, attn_sink, attn_w_out):
    l = x_prompt.shape[1]
    assert x_sample.shape[1] == l and l % 512 == 0
    bp, bs = x_prompt.shape[0], x_sample.shape[0]
    c_pad = jnp.zeros((8, D_MODEL), F32).at[:bp].set(c_prompt).at[bp:bp + bs].set(c_sample)
    mod = _ada_mod(c_pad, ada_w, ada_b)
    depth = mod.shape[0]
    mod = mod.reshape(depth, 8, 3 * N_SUB, D_MODEL)

    w_in16, w_out16 = _to_bf16(ffn_w_in), _to_bf16(ffn_w_out)
    w = dict(
        norm_g=norm_g, final_g=final_g,
        ffn_in=[[w_in16[i, s] for s in range(2)] for i in range(depth)],
        ffn_out=[[w_out16[i, s] for s in range(2)] for i in range(depth)],
        ab_wa=ab_w_in[0, :, :2 * D_A].astype(BF16),
        ab_wbt=ab_w_in[0, :, 2 * D_A:].T.astype(BF16),
        conv_w=conv_w[0], conv_b=conv_b[0], conv_ln_g=conv_ln_g[0], conv_ln_b=conv_ln_b[0],
        hy_short_w=hy_short_w[0], hy_short_b=hy_short_b[0], hy_skip=hy_skip[0],
        ab_woa=ab_w_out[0, :D_A].astype(BF16), ab_woz=ab_w_out[0, D_A:].astype(BF16),
        attn_w_qkv=attn_w_qkv[0].astype(BF16), attn_sink=attn_sink[0], attn_w_out=attn_w_out[0].astype(BF16),
    )
    tabs = _fft_tables(l)
    kt, ssq = _filt_gen(hy_w1[0], hy_b1[0], hy_w2[0], hy_b2[0], hy_w3[0], hy_freq[0], l)
    filt = _filt_fft(kt, ssq, tabs, l)

    y_prompt = _trunk(x_prompt, mod[:, :bp], w, filt, tabs)
    y_sample = _trunk(x_sample, mod[:, bp:bp + bs], w, filt, tabs)
    return (y_prompt, y_sample)
```

```python
import functools
import math

import numpy as np
import jax
import jax.numpy as jnp
from jax import lax
from jax.experimental import pallas as pl
from jax.experimental.pallas import tpu as pltpu

F32 = jnp.float32
BF16 = jnp.bfloat16

D_MODEL = 1024
D_FF = 2816
N_SUB = 3
EPS = 1e-6
D_A = 512
CONV_W = 31
CONV_HALO = 16
D_B = 512
HY_ORDER = 2
HY_BANDS = 16
HY_HIDDEN = 64
HY_FAST_PCT = 0.3
HY_SLOW_PCT = 1.5
HY_TARGET = 1e-2
N_HEADS = 16
N_KV = 4
GROUP = 4
HEAD_DIM = 64
WINDOW = 128
LOG2E = math.log2(math.e)

SUBLANES = 8
BF16_ROWS = 16
LANES = 128
VMEM_LIMIT = 56 * 1024 * 1024
HI = lax.Precision.HIGHEST


def _params(sem, vmem=VMEM_LIMIT):
    return pltpu.CompilerParams(dimension_semantics=sem, vmem_limit_bytes=vmem)


def _const_spec(shape):
    nd = len(shape)
    return pl.BlockSpec(shape, lambda *_: (0,) * nd, pipeline_mode=pl.Buffered(1))


def _silu(x):
    return x * jax.nn.sigmoid(x)


def _norm_mod(x, g, shift, scale):
    ms = jnp.mean(x * x, axis=-1, keepdims=True)
    return (x * lax.rsqrt(ms + EPS) * g) * (1.0 + scale) + shift


def _ada_kernel(c_ref, w_ref, b_ref, o_ref):
    cs = _silu(c_ref[...])
    o_ref[...] = jnp.dot(cs, w_ref[...], precision=HI, preferred_element_type=F32) + b_ref[...]


def _ada_mod(c_pad, ada_w, ada_b):
    depth, d, n = ada_w.shape
    tn = 1152
    return pl.pallas_call(
        _ada_kernel,
        grid=(depth, n // tn),
        in_specs=[pl.BlockSpec((8, d), lambda l, j: (0, 0)),
                  pl.BlockSpec((None, d, tn), lambda l, j: (l, 0, j)),
                  pl.BlockSpec((None, 1, tn), lambda l, j: (l, 0, j))],
        out_specs=pl.BlockSpec((None, 8, tn), lambda l, j: (l, 0, j)),
        out_shape=jax.ShapeDtypeStruct((depth, 8, n), F32),
        compiler_params=_params(("arbitrary", "arbitrary")),
        name="ada_mod",
    )(c_pad, ada_w, ada_b.reshape(depth, 1, n))


def _cast_kernel(x_ref, o_ref):
    o_ref[...] = x_ref[...].astype(o_ref.dtype)


def _to_bf16(w, *, block_bytes=8 * 1024 * 1024):
    shape = w.shape
    c = shape[-1]
    w2 = w.reshape(-1, c)
    r = w2.shape[0]
    tr = r
    while tr * c * 4 > block_bytes and tr % (2 * BF16_ROWS) == 0:
        tr //= 2
    out = pl.pallas_call(
        _cast_kernel,
        grid=(r // tr,),
        in_specs=[pl.BlockSpec((tr, c), lambda i: (i, 0))],
        out_specs=pl.BlockSpec((tr, c), lambda i: (i, 0)),
        out_shape=jax.ShapeDtypeStruct((r, c), BF16),
        compiler_params=_params(("arbitrary",)),
        name="cast_bf16",
    )(w2)
    return out.reshape(shape)


def _ffn_kernel(x_ref, mod_ref, g_ref, win_ref, wout_ref, fg_ref, o_ref, *, final, halves):
    rows = x_ref.shape[0] // halves
    for hf in range(halves):
        rs = slice(hf * rows, (hf + 1) * rows)
        x = x_ref[rs, :]
        h = _norm_mod(x, g_ref[...], mod_ref[0:1, :], mod_ref[1:2, :]).astype(BF16)
        gate = jnp.dot(h, win_ref[:, :D_FF], preferred_element_type=F32)
        up = jnp.dot(h, win_ref[:, D_FF:], preferred_element_type=F32)
        act = (_silu(gate) * up).astype(BF16)
        y = x + (0.5 * mod_ref[2:3, :]) * jnp.dot(act, wout_ref[...], preferred_element_type=F32)
        if final:
            y = y * lax.rsqrt(jnp.mean(y * y, axis=-1, keepdims=True) + EPS) * fg_ref[...]
        o_ref[rs, :] = y


def _ffn(x, mod3, norm_g, w_in, w_out, final_g, *, final, tm=1024, halves=2):
    b, l, d = x.shape
    kern = functools.partial(_ffn_kernel, final=final, halves=halves)
    return pl.pallas_call(
        kern,
        grid=(b, l // tm),
        in_specs=[pl.BlockSpec((None, tm, d), lambda bi, i: (bi, i, 0)),
                  pl.BlockSpec((None, 3, d), lambda bi, i: (bi, 0, 0)),
                  _const_spec((1, d)),
                  _const_spec(w_in.shape),
                  _const_spec(w_out.shape),
                  _const_spec((1, d))],
        out_specs=pl.BlockSpec((None, tm, d), lambda bi, i: (bi, i, 0)),
        out_shape=jax.ShapeDtypeStruct(x.shape, F32),
        compiler_params=_params(("arbitrary", "arbitrary")),
        name="ffn",
    )(x, mod3, norm_g.reshape(1, d), w_in, w_out, final_g.reshape(1, d))


def _conv_norm_swish(aext_ref, ash_ref, cw_ref, cb, lg, lb, r0, rows):
    span = rows + 2 * CONV_HALO - SUBLANES
    for r in range(1, SUBLANES):
        ash_ref[r - 1, 0:span, :] = aext_ref[r0 + r:r0 + r + span, :]
    acc = jnp.zeros((rows, D_A), F32) + cb
    off = CONV_HALO - CONV_W // 2
    for j in range(CONV_W):
        r, base = (off + j) % SUBLANES, (off + j) // SUBLANES * SUBLANES
        tap = aext_ref[r0 + base:r0 + base + rows, :] if r == 0 else ash_ref[r - 1, base:base + rows, :]
        acc = acc + cw_ref[j:j + 1, :] * tap
    mu = jnp.mean(acc, axis=-1, keepdims=True)
    cen = acc - mu
    var = jnp.mean(cen * cen, axis=-1, keepdims=True)
    return _silu(cen * lax.rsqrt(var + EPS) * lg + lb)


def _ffn_fused_kernel(*refs, pre, post, halves):
    refs = list(refs)
    take = lambda n: [refs.pop(0) for _ in range(n)]
    x_ref, mod_ref, g_ref, win_ref, wout_ref = take(5)
    if pre == "mix":
        ap_ref, ac_ref, an_ref, zt_ref, gmix_ref, woa_ref, woz_ref, cw_ref, cb_ref, lg_ref, lb_ref = take(11)
    if post == "proj":
        pmod_ref, pg_ref, wa_ref, wbt_ref = take(4)
        o_ref, a_ref, bpt_ref = take(3)
    elif post == "qkv":
        pmod_ref, pg_ref, wqkv_ref = take(3)
        o_ref, q_ref, k_ref, v_ref = take(4)
    else:
        (o_ref,) = take(1)
    if pre == "mix":
        aext_ref, ash_ref = take(2)
    assert not refs

    tm = x_ref.shape[0]
    rows = tm // halves
    i = pl.program_id(1)
    if pre == "mix":
        aext_ref[0:CONV_HALO, :] = jnp.where(i > 0, ap_ref[...], 0.0)
        aext_ref[CONV_HALO:CONV_HALO + tm, :] = ac_ref[...]
        aext_ref[CONV_HALO + tm:, :] = jnp.where(i < pl.num_programs(1) - 1, an_ref[...], 0.0)

    def start(hf):
        rs = slice(hf * rows, (hf + 1) * rows)
        x = x_ref[rs, :]
        if pre == "mix":
            a = _conv_norm_swish(aext_ref, ash_ref, cw_ref, cb_ref[...], lg_ref[...], lb_ref[...], hf * rows, rows)
            y = jnp.dot(a.astype(BF16), woa_ref[...], preferred_element_type=F32)
            y = y + lax.dot_general(zt_ref[:, rs], woz_ref[...], (((0,), (0,)), ((), ())),
                                    preferred_element_type=F32)
            x = x + gmix_ref[...] * y
        h = _norm_mod(x, g_ref[...], mod_ref[0:1, :], mod_ref[1:2, :]).astype(BF16)
        gate = jnp.dot(h, win_ref[:, :D_FF], preferred_element_type=F32)
        up = jnp.dot(h, win_ref[:, D_FF:], preferred_element_type=F32)
        return rs, x, gate, up

    def finish(st):
        rs, x, gate, up = st
        act = (_silu(gate) * up).astype(BF16)
        y = x + (0.5 * mod_ref[2:3, :]) * jnp.dot(act, wout_ref[...], preferred_element_type=F32)
        o_ref[rs, :] = y
        return rs, y

    def epilogue(done):
        rs, y = done
        if post is None:
            return
        h2 = _norm_mod(y, pg_ref[...], pmod_ref[0:1, :], pmod_ref[1:2, :]).astype(BF16)
        if post == "proj":
            pa = jnp.dot(h2, wa_ref[...], preferred_element_type=F32)
            a_ref[rs, :] = pa[:, :D_A] * jax.nn.sigmoid(pa[:, D_A:])
            bpt_ref[:, rs] = lax.dot_general(wbt_ref[...], h2, (((1,), (1,)), ((), ())),
                                             preferred_element_type=F32)
        else:
            qkv = jnp.dot(h2, wqkv_ref[...], preferred_element_type=F32)
            nq, nk = N_HEADS * HEAD_DIM, N_KV * HEAD_DIM
            q_ref[rs, :] = (qkv[:, :nq] * (HEAD_DIM ** -0.5 * LOG2E)).astype(BF16)

            def with_swapped(t):
                sw = [pltpu.roll(t[:, j * LANES:(j + 1) * LANES], HEAD_DIM, axis=1) for j in range(nk // LANES)]
                return jnp.concatenate([t] + sw, axis=1).astype(BF16)

            k_ref[rs, :] = with_swapped(qkv[:, nq:nq + nk])
            v_ref[rs, :] = with_swapped(qkv[:, nq + nk:])

    st = start(0)
    for hf in range(halves):
        done = finish(st)
        if hf + 1 < halves:
            st = start(hf + 1)
        epilogue(done)


def _ffn_fused(x, mod3, norm_g, w_in, w_out, *, pre=None, post=None, tm=512, halves=2):
    b, l, d = x.shape
    tile = lambda w: pl.BlockSpec((None, tm, w), lambda bi, i: (bi, i, 0))
    per_seq = lambda r: pl.BlockSpec((None, r, d), lambda bi, i: (bi, 0, 0))
    args = [x, mod3, norm_g.reshape(1, d), w_in, w_out]
    specs = [tile(d), per_seq(3), _const_spec((1, d)), _const_spec(w_in.shape), _const_spec(w_out.shape)]
    scratch = []
    pre_kind = post_kind = None
    if pre is not None:
        pre_kind, a_glu, zt, gate_mix, woa, woz, conv_w, conv_b, ln_g, ln_b = pre
        nh, nblk = tm // CONV_HALO, l // CONV_HALO
        args += [a_glu, a_glu, a_glu, zt, gate_mix, woa, woz, conv_w, conv_b.reshape(1, D_A), ln_g.reshape(1, D_A),
                 ln_b.reshape(1, D_A)]
        specs += [pl.BlockSpec((None, CONV_HALO, D_A), lambda bi, i: (bi, jnp.maximum(i * nh - 1, 0), 0)),
                  tile(D_A),
                  pl.BlockSpec((None, CONV_HALO, D_A), lambda bi, i: (bi, jnp.minimum((i + 1) * nh, nblk - 1), 0)),
                  pl.BlockSpec((None, D_B, tm), lambda bi, i: (bi, 0, i)),
                  per_seq(1), _const_spec(woa.shape), _const_spec(woz.shape), _const_spec(conv_w.shape),
                  _const_spec((1, D_A)), _const_spec((1, D_A)), _const_spec((1, D_A))]
        rows = tm // halves
        scratch = [pltpu.VMEM((tm + 2 * CONV_HALO, D_A), F32),
                   pltpu.VMEM((SUBLANES - 1, rows + 2 * CONV_HALO - SUBLANES, D_A), F32)]
    out_specs = [tile(d)]
    out_shape = [jax.ShapeDtypeStruct(x.shape, F32)]
    if post is not None:
        post_kind = post[0]
        args += [post[1], post[2].reshape(1, d)] + list(post[3:])
        specs += [per_seq(3), _const_spec((1, d))] + [_const_spec(w.shape) for w in post[3:]]
        if post_kind == "proj":
            out_specs += [tile(D_A), pl.BlockSpec((None, 3 * D_B, tm), lambda bi, i: (bi, 0, i))]
            out_shape += [jax.ShapeDtypeStruct((b, l, D_A), F32), jax.ShapeDtypeStruct((b, 3 * D_B, l), F32)]
        else:
            nq, nk = N_HEADS * HEAD_DIM, N_KV * HEAD_DIM
            out_specs += [tile(nq), tile(2 * nk), tile(2 * nk)]
            out_shape += [jax.ShapeDtypeStruct((b, l, nq), BF16), jax.ShapeDtypeStruct((b, l, 2 * nk), BF16),
                          jax.ShapeDtypeStruct((b, l, 2 * nk), BF16)]
    kern = functools.partial(_ffn_fused_kernel, pre=pre_kind, post=post_kind, halves=halves)
    return pl.pallas_call(
        kern,
        grid=(b, l // tm),
        in_specs=specs,
        out_specs=out_specs,
        out_shape=out_shape,
        scratch_shapes=scratch,
        compiler_params=_params(("arbitrary", "arbitrary")),
        name="ffn_" + "_".join(k for k in (pre_kind, post_kind) if k),
    )(*args)


def _fft_tables(l):
    n = 2 * l
    n1 = n // LANES
    h = n1 // 2
    nr = min(n1, -(-(h + 1) // BF16_ROWS) * BF16_ROWS)
    k1 = np.arange(nr)[:, None].astype(np.float64)
    r1 = np.arange(n1)[None, :].astype(np.float64)
    ang1 = 2.0 * np.pi * k1 * r1 / n1
    f1 = np.concatenate([np.cos(ang1), -np.sin(ang1)], axis=0)
    n2 = np.arange(LANES)[None, :].astype(np.float64)
    angt = 2.0 * np.pi * k1 * n2 / n
    tc = np.concatenate([np.cos(angt), np.cos(angt)], axis=1)
    ts = np.concatenate([np.sin(angt), -np.sin(angt)], axis=1)
    a2 = 2.0 * np.pi * np.arange(LANES)[:, None] * np.arange(LANES)[None, :] / LANES
    c2, s2 = np.cos(a2), -np.sin(a2)
    f2f = np.block([[c2, s2], [-s2, c2]])
    f2i = np.block([[c2, -s2], [s2, c2]])
    ang4 = 2.0 * np.pi * np.arange(h)[:, None] * np.arange(h)[None, :] / n1
    wgt = np.where(np.arange(h)[None, :] == 0, 1.0, 2.0)
    g4 = np.concatenate([wgt * np.cos(ang4), -wgt * np.sin(ang4)], axis=1)
    sgn = np.broadcast_to(((-1.0) ** np.arange(h))[:, None], (h, 2 * LANES))
    f32 = lambda a: jnp.asarray(a, F32)
    return dict(f1=f32(f1), f1h=f32(f1[:, :n1 // 2]), tc=f32(tc), ts=f32(ts), f2f=f32(f2f), f2i=f32(f2i),
                g4=f32(g4), sgn=f32(sgn))


def _swap(a):
    return jnp.concatenate([a[..., LANES:], a[..., :LANES]], axis=-1)


def _stage1_pair(f1, z0, z1, tc, ts):
    zp = jnp.concatenate([z0, z1], axis=1).astype(BF16)
    aa = jnp.dot(f1, zp, preferred_element_type=F32)
    nr = aa.shape[0] // 2
    outs = []
    for q in range(2):
        a = jnp.concatenate([aa[:nr, q * LANES:(q + 1) * LANES], aa[nr:, q * LANES:(q + 1) * LANES]], axis=1)
        a = a.astype(BF16)
        outs.append(a * tc + _swap(a) * ts)
    return outs


def _filt_gen_kernel(w1t_ref, b1_ref, w2t_ref, b2_ref, fr_ref, w3t_ref, dl_ref, k_ref, ssq_ref, *, l, tt):
    i = pl.program_id(0)
    inv_lm1 = 1.0 / (l - 1)
    n = i * tt + lax.broadcasted_iota(jnp.int32, (1, tt), 1)
    j = jnp.where(n < l, n, 2 * l - n).astype(F32)
    t = j * inv_lm1
    w = (2.0 * math.pi) * j / l
    bands = 1e-4 + lax.broadcasted_iota(jnp.int32, (HY_BANDS, 1), 0).astype(F32) * ((HY_BANDS - 1 - 1e-4) / (HY_BANDS - 1))
    fw = bands * w
    fr = fr_ref[...]
    z1 = (w1t_ref[:, 0:1] * t
          + jnp.dot(w1t_ref[:, 1:1 + HY_BANDS], jnp.cos(fw), precision=HI, preferred_element_type=F32)
          - jnp.dot(w1t_ref[:, 1 + HY_BANDS:], jnp.sin(fw), precision=HI, preferred_element_type=F32)
          + b1_ref[...])
    h1 = jnp.sin(fr * z1)
    h2 = jnp.sin(fr * (jnp.dot(w2t_ref[...], h1, precision=HI, preferred_element_type=F32) + b2_ref[...]))
    ht = jnp.dot(w3t_ref[...], h2.astype(BF16), preferred_element_type=F32)
    dec = jnp.exp(-t * dl_ref[...])
    dec = jnp.where(n == l, 0.0, dec)
    k = ht * jnp.concatenate([dec] * HY_ORDER, axis=0)
    k_ref[...] = k.astype(k_ref.dtype)

    @pl.when(i == 0)
    def _():
        ssq_ref[...] = jnp.zeros_like(ssq_ref)

    k2 = k * k
    part = k2[:, 0:LANES]
    for q in range(1, tt // LANES):
        part = part + k2[:, q * LANES:(q + 1) * LANES]
    ssq_ref[...] += part


def _filt_gen(hw1, hb1, hw2, hb2, hw3, hfreq, l, *, tt=1024):
    n = 2 * l
    nt = n // tt
    rows = HY_ORDER * D_B
    w3t = hw3.T.reshape(2, rows, HY_HIDDEN).astype(BF16)
    max_decay = math.log(HY_TARGET) / HY_FAST_PCT
    min_decay = math.log(HY_TARGET) / HY_SLOW_PCT
    deltas = jnp.abs(jnp.linspace(min_decay, max_decay, D_B, dtype=F32)).reshape(D_B, 1)
    col = lambda a: a.reshape(HY_HIDDEN, 1)
    kern = functools.partial(_filt_gen_kernel, l=l, tt=tt)
    return pl.pallas_call(
        kern,
        grid=(nt,),
        in_specs=[_const_spec((HY_HIDDEN, 2 * HY_BANDS + 1)), _const_spec((HY_HIDDEN, 1)),
                  _const_spec((HY_HIDDEN, HY_HIDDEN)), _const_spec((HY_HIDDEN, 1)), _const_spec((HY_HIDDEN, 1)),
                  pl.BlockSpec((None, rows, HY_HIDDEN), lambda i: (jnp.where(i * tt >= l, 1, 0), 0, 0)),
                  _const_spec((D_B, 1))],
        out_specs=[pl.BlockSpec((rows, tt), lambda i: (0, i)),
                   pl.BlockSpec((rows, LANES), lambda i: (0, 0))],
        out_shape=[jax.ShapeDtypeStruct((rows, n), BF16), jax.ShapeDtypeStruct((rows, LANES), F32)],
        compiler_params=_params(("arbitrary",)),
        name="hyena_filter_gen",
    )(hw1.T, col(hb1), hw2.T, col(hb2), col(hfreq), w3t, deltas)


def _filt_fft_kernel(k_ref, ssq_ref, f1_ref, tc_ref, ts_ref, f2_ref, o_ref, *, ct, n_total):
    tc, ts = tc_ref[...].astype(BF16), ts_ref[...].astype(BF16)
    f1, f2 = f1_ref[...].astype(BF16), f2_ref[...].astype(BF16)
    nr = tc.shape[0]
    b = []
    for p in range(ct // 2):
        b0, b1 = _stage1_pair(f1, k_ref[2 * p], k_ref[2 * p + 1], tc, ts)
        b.append(jnp.concatenate([b0, b1], axis=0))
    x = [jnp.dot(bp, f2, preferred_element_type=F32) for bp in b]
    for p in range(ct // 2):
        for q in range(2):
            c = 2 * p + q
            tot = jnp.sum(ssq_ref[c], axis=-1, keepdims=True)
            o_ref[c] = (x[p][q * nr:(q + 1) * nr] * (lax.rsqrt(tot + EPS) * (1.0 / n_total))).astype(o_ref.dtype)


def _filt_fft(k, ssq, tabs, l, *, ct=8):
    rows = k.shape[0]
    n = 2 * l
    n1 = n // LANES
    nr = tabs["tc"].shape[0]
    kern = functools.partial(_filt_fft_kernel, ct=ct, n_total=n)
    return pl.pallas_call(
        kern,
        grid=(rows // ct,),
        in_specs=[pl.BlockSpec((ct, n1, LANES), lambda c: (c, 0, 0)),
                  pl.BlockSpec((ct, 1, LANES), lambda c: (c, 0, 0)),
                  _const_spec(tabs["f1"].shape), _const_spec(tabs["tc"].shape), _const_spec(tabs["ts"].shape),
                  _const_spec(tabs["f2f"].shape)],
        out_specs=pl.BlockSpec((ct, nr, 2 * LANES), lambda c: (c, 0, 0)),
        out_shape=jax.ShapeDtypeStruct((rows, nr, 2 * LANES), BF16),
        compiler_params=_params(("arbitrary",)),
        name="hyena_filter_fft",
    )(k.reshape(rows, n1, LANES), ssq.reshape(rows, 1, LANES), tabs["f1"], tabs["tc"], tabs["ts"], tabs["f2f"])


def _shift_prev(a, row, lane):
    nv = a.shape[0] // SUBLANES
    r3 = pltpu.roll(a, 1, axis=1).reshape(nv, SUBLANES, LANES)
    rot = pltpu.roll(r3, 1, axis=1)
    rot_prev = jnp.concatenate([jnp.zeros((1, SUBLANES, LANES), a.dtype), rot[:-1]], axis=0)
    out = jnp.where((lane == 0) & (row == 0), rot_prev, jnp.where(lane == 0, rot, r3))
    return out.reshape(a.shape)


def _shift_next(a, row, lane):
    nv = a.shape[0] // SUBLANES
    r3 = pltpu.roll(a, LANES - 1, axis=1).reshape(nv, SUBLANES, LANES)
    rot = pltpu.roll(r3, SUBLANES - 1, axis=1)
    rot_next = jnp.concatenate([rot[1:], jnp.zeros((1, SUBLANES, LANES), a.dtype)], axis=0)
    out = jnp.where((lane == LANES - 1) & (row == SUBLANES - 1), rot_next, jnp.where(lane == LANES - 1, rot, r3))
    return out.reshape(a.shape)


def _hyena_kernel(sw_ref, sb_ref, skip_ref, bp_ref, kf_ref, f1_ref, tc_ref, ts_ref, f2f_ref, f2i_ref, g4_ref,
                  sgn_ref, o_ref, *, ct):
    ci = pl.program_id(1)
    nh = bp_ref.shape[2]
    nr = tc_ref.shape[0]
    npair = ct // 2
    row = lax.broadcasted_iota(jnp.int32, (nh // SUBLANES, SUBLANES, LANES), 1)
    lane = lax.broadcasted_iota(jnp.int32, (nh // SUBLANES, SUBLANES, LANES), 2)
    tc, ts = tc_ref[...].astype(BF16), ts_ref[...].astype(BF16)
    f1, f2f, f2i, g4 = (r[...].astype(BF16) for r in (f1_ref, f2f_ref, f2i_ref, g4_ref))

    def short_conv(part, c):
        ch = part * D_B + ci * ct + c
        a = bp_ref[part, c]
        return (sw_ref[0, ch] * _shift_prev(a, row, lane) + sw_ref[1, ch] * a
                + sw_ref[2, ch] * _shift_next(a, row, lane) + sb_ref[ch])

    def stage1(z0, z1):
        b0, b1 = _stage1_pair(f1, z0, z1, tc, ts)
        return jnp.concatenate([b0, b1], axis=0)

    def stage2(b):
        return jnp.dot(b, f2f, preferred_element_type=F32)

    def filt(x, order, p):
        outs = []
        for q in range(2):
            kf = kf_ref[order, 2 * p + q]
            kr, ki = kf[:, :LANES], kf[:, LANES:]
            xq = x[q * nr:(q + 1) * nr].astype(BF16)
            outs.append(xq * jnp.concatenate([kr, kr], axis=-1) + _swap(xq) * jnp.concatenate([-ki, ki], axis=-1))
        return jnp.concatenate(outs, axis=0)

    def stage3(y):
        return jnp.dot(y, f2i, preferred_element_type=F32)

    def twiddle_inv(cm):
        cm = cm.astype(BF16)
        d0 = cm[:nr] * tc - _swap(cm[:nr]) * ts
        d1 = cm[nr:] * tc - _swap(cm[nr:]) * ts
        rhs = jnp.concatenate(
            [jnp.concatenate([d0[:nh, :LANES], d1[:nh, :LANES]], axis=1),
             jnp.concatenate([d0[:nh, LANES:], d1[:nh, LANES:]], axis=1)], axis=0)
        mid = jnp.concatenate([d0[nh:nh + 1, :LANES], d1[nh:nh + 1, :LANES]], axis=1).astype(F32)
        return rhs, mid

    def stage4(d):
        rhs, mid = d
        return jnp.dot(g4, rhs, preferred_element_type=F32) + sgn_ref[...] * mid

    def long_conv(zs, order):
        b = [stage1(*zs[p]) for p in range(npair)]
        x = [stage2(b[p]) for p in range(npair)]
        y = [filt(x[p], order, p) for p in range(npair)]
        c = [stage3(y[p]) for p in range(npair)]
        d = [twiddle_inv(c[p]) for p in range(npair)]
        o = [stage4(d[p]) for p in range(npair)]
        return [(o[p][:, :LANES], o[p][:, LANES:]) for p in range(npair)]

    v = [(short_conv(0, 2 * p), short_conv(0, 2 * p + 1)) for p in range(npair)]
    y1 = long_conv(v, 0)
    z1 = []
    for p in range(npair):
        pr = []
        for q in range(2):
            c = 2 * p + q
            pr.append(short_conv(1, c) * (y1[p][q] + v[p][q] * skip_ref[0, ci * ct + c]))
        z1.append(tuple(pr))
    y2 = long_conv(z1, 1)
    for p in range(npair):
        for q in range(2):
            c = 2 * p + q
            o_ref[c] = (short_conv(2, c) * (y2[p][q] + z1[p][q] * skip_ref[1, ci * ct + c])).astype(o_ref.dtype)


def _hyena(bpt, kf, short_w, short_b, skip, tabs, *, ct=8):
    b, _, l = bpt.shape
    nh = l // LANES
    nr = tabs["tc"].shape[0]
    smem = pl.BlockSpec(memory_space=pltpu.SMEM)
    kern = functools.partial(_hyena_kernel, ct=ct)
    out = pl.pallas_call(
        kern,
        grid=(b, D_B // ct),
        in_specs=[smem, smem, smem,
                  pl.BlockSpec((None, 3, ct, nh, LANES), lambda bi, c: (bi, 0, c, 0, 0)),
                  pl.BlockSpec((HY_ORDER, ct, nr, 2 * LANES), lambda bi, c: (0, c, 0, 0)),
                  _const_spec(tabs["f1h"].shape), _const_spec(tabs["tc"].shape), _const_spec(tabs["ts"].shape),
                  _const_spec(tabs["f2f"].shape), _const_spec(tabs["f2i"].shape), _const_spec(tabs["g4"].shape),
                  _const_spec(tabs["sgn"].shape)],
        out_specs=pl.BlockSpec((None, ct, nh, LANES), lambda bi, c: (bi, c, 0, 0)),
        out_shape=jax.ShapeDtypeStruct((b, D_B, nh, LANES), BF16),
        compiler_params=_params(("arbitrary", "arbitrary")),
        name="hyena_conv",
    )(short_w, short_b, skip, bpt.reshape(b, 3, D_B, nh, LANES), kf.reshape(HY_ORDER, D_B, nr, 2 * LANES),
      tabs["f1h"], tabs["tc"], tabs["ts"], tabs["f2f"], tabs["f2i"], tabs["g4"], tabs["sgn"])
    return out.reshape(b, D_B, l)


def _alibi_slopes():
    return [2.0 ** (-8.0 * (i + 1) / N_HEADS) for i in range(N_HEADS)]


def _attn_kernel(sink_ref, b0_ref, b1_ref, b2_ref, x_ref, q_ref, kp_ref, kc_ref, kn_ref, vp_ref, vc_ref, vn_ref,
                 mod_ref, wo_ref, o_ref, kall_ref, vall_ref, oh_ref, *, tq, ahead):
    i = pl.program_id(1)
    nk = N_KV * HEAD_DIM
    kall_ref[0:WINDOW, :] = kp_ref[...]
    kall_ref[WINDOW:WINDOW + tq, :] = kc_ref[...]
    kall_ref[WINDOW + tq:, :] = kn_ref[...]
    vall_ref[0:WINDOW, :] = vp_ref[...]
    vall_ref[WINDOW:WINDOW + tq, :] = vc_ref[...]
    vall_ref[WINDOW + tq:, :] = vn_ref[...]

    span = 3 * WINDOW
    lane = lax.broadcasted_iota(jnp.int32, (WINDOW, LANES), 1)
    low = lane < HEAD_DIM

    def sub_block(sb, carry):
        r0 = pl.multiple_of(sb * WINDOW, WINDOW)
        first = (i == 0) & (sb == 0)
        lastb = (i == pl.num_programs(1) - 1) & (sb == tq // WINDOW - 1)

        def cols_for(head):
            kvh = head // GROUP
            tile = kvh // 2
            plain = slice(tile * LANES, (tile + 1) * LANES)
            swapped = slice(nk + tile * LANES, nk + (tile + 1) * LANES)
            return plain if kvh % 2 == head % 2 else swapped

        def scores(head):
            pair = head // 2
            q2 = q_ref[pl.ds(r0, WINDOW), pair * LANES:(pair + 1) * LANES]
            qz = jnp.zeros_like(q2)
            qm = jnp.where(low, q2, qz) if head % 2 == 0 else jnp.where(low, qz, q2)
            kw = kall_ref[pl.ds(r0, span), cols_for(head)]
            s = lax.dot_general(qm, kw, (((1,), (1,)), ((), ())), preferred_element_type=F32)
            bias = jnp.concatenate([b0_ref[jnp.where(first, N_HEADS, head)], b1_ref[head],
                                    b2_ref[jnp.where(lastb, N_HEADS, head)]], axis=1)
            return s + bias

        pend = [scores(h) for h in range(ahead)]
        outs = []
        for head in range(N_HEADS):
            s = pend.pop(0)
            if head + ahead < N_HEADS:
                pend.append(scores(head + ahead))
            vw = vall_ref[pl.ds(r0, span), cols_for(head)]
            snk = sink_ref[head] * LOG2E
            m = jnp.maximum(jnp.max(s, axis=-1, keepdims=True), snk)
            p = jnp.exp2(s - m)
            den = jnp.sum(p, axis=-1, keepdims=True) + jnp.exp2(snk - m)
            outs.append(jnp.dot(p.astype(BF16), vw, preferred_element_type=F32) / den)
            if head % 2 == 1:
                pair = head // 2
                oh_ref[pl.ds(r0, WINDOW), pair * LANES:(pair + 1) * LANES] = (
                    jnp.where(low, outs[0], outs[1]).astype(BF16))
                outs = []
        return carry

    lax.fori_loop(0, tq // WINDOW, sub_block, 0)
    y = jnp.dot(oh_ref[...], wo_ref[...], preferred_element_type=F32)
    o_ref[...] = x_ref[...] + mod_ref[2:3, :] * y


def _attn(x, q, kk, vv, mod3, sink, w_out, *, tq=512, ahead=4):
    b, l, d = x.shape
    nq, nk2 = q.shape[-1], kk.shape[-1]
    r = tq // WINDOW
    nblk = l // WINDOW
    prev = lambda bi, i: (bi, jnp.maximum(i * r - 1, 0), 0)
    cur = lambda bi, i: (bi, i, 0)
    nxt = lambda bi, i: (bi, jnp.minimum((i + 1) * r, nblk - 1), 0)
    dist = np.abs(np.arange(3 * WINDOW)[None, :] - WINDOW - np.arange(WINDOW)[:, None])
    bias = -np.asarray(_alibi_slopes())[:, None, None] * dist[None] * LOG2E
    bias = np.where(dist[None] <= WINDOW, bias, -np.inf)
    gone = np.full((1, WINDOW, WINDOW), -np.inf)
    b0 = jnp.asarray(np.concatenate([bias[:, :, :WINDOW], gone]), F32)
    b1 = jnp.asarray(bias[:, :, WINDOW:2 * WINDOW], F32)
    b2 = jnp.asarray(np.concatenate([bias[:, :, 2 * WINDOW:], gone]), F32)
    kern = functools.partial(_attn_kernel, tq=tq, ahead=ahead)
    return pl.pallas_call(
        kern,
        grid=(b, l // tq),
        in_specs=[pl.BlockSpec(memory_space=pltpu.SMEM),
                  _const_spec(b0.shape), _const_spec(b1.shape), _const_spec(b2.shape),
                  pl.BlockSpec((None, tq, d), cur),
                  pl.BlockSpec((None, tq, nq), cur),
                  pl.BlockSpec((None, WINDOW, nk2), prev), pl.BlockSpec((None, tq, nk2), cur),
                  pl.BlockSpec((None, WINDOW, nk2), nxt),
                  pl.BlockSpec((None, WINDOW, nk2), prev), pl.BlockSpec((None, tq, nk2), cur),
                  pl.BlockSpec((None, WINDOW, nk2), nxt),
                  pl.BlockSpec((None, 3, d), lambda bi, i: (bi, 0, 0)),
                  _const_spec(w_out.shape)],
        out_specs=pl.BlockSpec((None, tq, d), cur),
        out_shape=jax.ShapeDtypeStruct(x.shape, F32),
        scratch_shapes=[pltpu.VMEM((tq + 2 * WINDOW, nk2), BF16), pltpu.VMEM((tq + 2 * WINDOW, nk2), BF16),
                        pltpu.VMEM((tq, nq), BF16)],
        compiler_params=_params(("arbitrary", "arbitrary")),
        name="attn",
    )(sink, b0, b1, b2, x, q, kk, kk, kk, vv, vv, vv, mod3, w_out)


def _trunk(x, mods, w, filt, tabs):
    m = [mods[0][:, 3 * s:3 * s + 3] for s in range(N_SUB)]
    x, a_glu, bpt = _ffn_fused(x, m[0], w["norm_g"][0, 0], w["ffn_in"][0][0], w["ffn_out"][0][0],
                               post=("proj", m[1], w["norm_g"][0, 1], w["ab_wa"], w["ab_wbt"]))
    zt = _hyena(bpt, filt, w["hy_short_w"], w["hy_short_b"], w["hy_skip"], tabs)
    x = _ffn_fused(x, m[2], w["norm_g"][0, 2], w["ffn_in"][0][1], w["ffn_out"][0][1],
                   pre=("mix", a_glu, zt, m[1][:, 2:3], w["ab_woa"], w["ab_woz"], w["conv_w"], w["conv_b"],
                        w["conv_ln_g"], w["conv_ln_b"]))[0]

    m = [mods[1][:, 3 * s:3 * s + 3] for s in range(N_SUB)]
    x, q, kk, vv = _ffn_fused(x, m[0], w["norm_g"][1, 0], w["ffn_in"][1][0], w["ffn_out"][1][0],
                              post=("qkv", m[1], w["norm_g"][1, 1], w["attn_w_qkv"]))
    x = _attn(x, q, kk, vv, m[1], w["attn_sink"], w["attn_w_out"])
    x = _ffn(x, m[2], w["norm_g"][1, 2], w["ffn_in"][1][1], w["ffn_out"][1][1], w["final_g"], final=True)
    return x


def kernel(x_prompt, x_sample, c_prompt, c_sample, ada_w, ada_b, norm_g, ffn_w_in, ffn_w_out, final_g, ab_w_in,
           conv_w, conv_b, conv_ln_g, conv_ln_b, hy_short_w, hy_short_b, hy_w1, hy_b1, hy_w2, hy_b2, hy_w3,
           hy_freq, hy_skip, ab_w_out, attn_w_qkv, attn_sink, attn_w_out):
    l = x_prompt.shape[1]
    assert x_sample.shape[1] == l and l % 512 == 0
    bp, bs = x_prompt.shape[0], x_sample.shape[0]
    c_pad = jnp.zeros((8, D_MODEL), F32).at[:bp].set(c_prompt).at[bp:bp + bs].set(c_sample)
    mod = _ada_mod(c_pad, ada_w, ada_b)
    depth = mod.shape[0]
    mod = mod.reshape(depth, 8, 3 * N_SUB, D_MODEL)

    w_in16, w_out16 = _to_bf16(ffn_w_in), _to_bf16(ffn_w_out)
    w = dict(
        norm_g=norm_g, final_g=final_g,
        ffn_in=[[w_in16[i, s] for s in range(2)] for i in range(depth)],
        ffn_out=[[w_out16[i, s] for s in range(2)] for i in range(depth)],
        ab_wa=ab_w_in[0, :, :2 * D_A].astype(BF16),
        ab_wbt=ab_w_in[0, :, 2 * D_A:].T.astype(BF16),
        conv_w=conv_w[0], conv_b=conv_b[0], conv_ln_g=conv_ln_g[0], conv_ln_b=conv_ln_b[0],
        hy_short_w=hy_short_w[0], hy_short_b=hy_short_b[0], hy_skip=hy_skip[0],
        ab_woa=ab_w_out[0, :D_A].astype(BF16), ab_woz=ab_w_out[0, D_A:].astype(BF16),
        attn_w_qkv=attn_w_qkv[0].astype(BF16), attn_sink=attn_sink[0], attn_w_out=attn_w_out[0].astype(BF16),
    )
    tabs = _fft_tables(l)
    kt, ssq = _filt_gen(hy_w1[0], hy_b1[0], hy_w2[0], hy_b2[0], hy_w3[0], hy_freq[0], l)
    filt = _filt_fft(kt, ssq, tabs, l)

    y_prompt = _trunk(x_prompt, mod[:, :bp], w, filt, tabs)
    y_sample = _trunk(x_sample, mod[:, bp:bp + bs], w, filt, tabs)
    return (y_prompt, y_sample)
```

```python
import functools
import math

import numpy as np
import jax
import jax.numpy as jnp
from jax import lax
from jax.experimental import pallas as pl
from jax.experimental.pallas import tpu as pltpu

F32 = jnp.float32
BF16 = jnp.bfloat16

D_MODEL = 1024
D_FF = 2816
N_SUB = 3
EPS = 1e-6
D_A = 512
CONV_W = 31
CONV_HALO = 16
D_B = 512
HY_ORDER = 2
HY_BANDS = 16
HY_HIDDEN = 64
HY_FAST_PCT = 0.3
HY_SLOW_PCT = 1.5
HY_TARGET = 1e-2
N_HEADS = 16
N_KV = 4
GROUP = 4
HEAD_DIM = 64
WINDOW = 128
LOG2E = math.log2(math.e)

SUBLANES = 8
BF16_ROWS = 16
LANES = 128
VMEM_LIMIT = 56 * 1024 * 1024
HI = lax.Precision.HIGHEST


def _params(sem, vmem=VMEM_LIMIT):
    return pltpu.CompilerParams(dimension_semantics=sem, vmem_limit_bytes=vmem)


def _const_spec(shape):
    nd = len(shape)
    return pl.BlockSpec(shape, lambda *_: (0,) * nd, pipeline_mode=pl.Buffered(1))


def _silu(x):
    return x * jax.nn.sigmoid(x)


def _norm_mod(x, g, shift, scale):
    ms = jnp.mean(x * x, axis=-1, keepdims=True)
    return (x * lax.rsqrt(ms + EPS) * g) * (1.0 + scale) + shift


def _ada_kernel(c_ref, w_ref, b_ref, o_ref):
    cs = _silu(c_ref[...])
    o_ref[...] = jnp.dot(cs, w_ref[...], precision=HI, preferred_element_type=F32) + b_ref[...]


def _ada_mod(c_pad, ada_w, ada_b):
    depth, d, n = ada_w.shape
    tn = 1152
    return pl.pallas_call(
        _ada_kernel,
        grid=(depth, n // tn),
        in_specs=[pl.BlockSpec((8, d), lambda l, j: (0, 0)),
                  pl.BlockSpec((None, d, tn), lambda l, j: (l, 0, j)),
                  pl.BlockSpec((None, 1, tn), lambda l, j: (l, 0, j))],
        out_specs=pl.BlockSpec((None, 8, tn), lambda l, j: (l, 0, j)),
        out_shape=jax.ShapeDtypeStruct((depth, 8, n), F32),
        compiler_params=_params(("arbitrary", "arbitrary")),
        name="ada_mod",
    )(c_pad, ada_w, ada_b.reshape(depth, 1, n))


def _cast_kernel(x_ref, o_ref):
    o_ref[...] = x_ref[...].astype(o_ref.dtype)


def _to_bf16(w, *, block_bytes=8 * 1024 * 1024):
    shape = w.shape
    c = shape[-1]
    w2 = w.reshape(-1, c)
    r = w2.shape[0]
    tr = r
    while tr * c * 4 > block_bytes and tr % (2 * BF16_ROWS) == 0:
        tr //= 2
    out = pl.pallas_call(
        _cast_kernel,
        grid=(r // tr,),
        in_specs=[pl.BlockSpec((tr, c), lambda i: (i, 0))],
        out_specs=pl.BlockSpec((tr, c), lambda i: (i, 0)),
        out_shape=jax.ShapeDtypeStruct((r, c), BF16),
        compiler_params=_params(("arbitrary",)),
        name="cast_bf16",
    )(w2)
    return out.reshape(shape)


def _mixer_weights_kernel(abin_ref, about_ref, qkv_ref, wo_ref, wa_ref, wbt_ref, woa_ref, woz_ref, wqkv_ref, wo16_ref):
    wa_ref[...] = abin_ref[:, :2 * D_A].astype(BF16)
    wbt_ref[...] = abin_ref[:, 2 * D_A:].T.astype(BF16)
    woa_ref[...] = about_ref[:D_A, :].astype(BF16)
    woz_ref[...] = about_ref[D_A:, :].astype(BF16)
    wqkv_ref[...] = qkv_ref[...].astype(BF16)
    wo16_ref[...] = wo_ref[...].astype(BF16)


def _mixer_weights(ab_w_in, ab_w_out, attn_w_qkv, attn_w_out):
    first = lambda w: pl.BlockSpec((None,) + w.shape[1:], lambda i: (0,) * w.ndim, pipeline_mode=pl.Buffered(1))
    d = ab_w_in.shape[1]
    nqkv = attn_w_qkv.shape[-1]
    shapes = [(d, 2 * D_A), (3 * D_B, d), (D_A, d), (D_B, d), (d, nqkv), attn_w_out.shape[1:]]
    return pl.pallas_call(
        _mixer_weights_kernel,
        grid=(1,),
        in_specs=[first(ab_w_in), first(ab_w_out), first(attn_w_qkv), first(attn_w_out)],
        out_specs=[_const_spec(s) for s in shapes],
        out_shape=[jax.ShapeDtypeStruct(s, BF16) for s in shapes],
        compiler_params=_params(("arbitrary",)),
        name="mixer_weights",
    )(ab_w_in, ab_w_out, attn_w_qkv, attn_w_out)


def _ffn_kernel(x_ref, mod_ref, g_ref, win_ref, wout_ref, fg_ref, o_ref, *, final, halves):
    rows = x_ref.shape[0] // halves
    for hf in range(halves):
        rs = slice(hf * rows, (hf + 1) * rows)
        x = x_ref[rs, :]
        h = _norm_mod(x, g_ref[...], mod_ref[0:1, :], mod_ref[1:2, :]).astype(BF16)
        gate = jnp.dot(h, win_ref[:, :D_FF], preferred_element_type=F32)
        up = jnp.dot(h, win_ref[:, D_FF:], preferred_element_type=F32)
        act = (_silu(gate) * up).astype(BF16)
        y = x + (0.5 * mod_ref[2:3, :]) * jnp.dot(act, wout_ref[...], preferred_element_type=F32)
        if final:
            y = y * lax.rsqrt(jnp.mean(y * y, axis=-1, keepdims=True) + EPS) * fg_ref[...]
        o_ref[rs, :] = y


def _ffn(x, mod3, norm_g, w_in, w_out, final_g, *, final, tm=1024, halves=2):
    b, l, d = x.shape
    kern = functools.partial(_ffn_kernel, final=final, halves=halves)
    return pl.pallas_call(
        kern,
        grid=(b, l // tm),
        in_specs=[pl.BlockSpec((None, tm, d), lambda bi, i: (bi, i, 0)),
                  pl.BlockSpec((None, 3, d), lambda bi, i: (bi, 0, 0)),
                  _const_spec((1, d)),
                  _const_spec(w_in.shape),
                  _const_spec(w_out.shape),
                  _const_spec((1, d))],
        out_specs=pl.BlockSpec((None, tm, d), lambda bi, i: (bi, i, 0)),
        out_shape=jax.ShapeDtypeStruct(x.shape, F32),
        compiler_params=_params(("arbitrary", "arbitrary")),
        name="ffn",
    )(x, mod3, norm_g.reshape(1, d), w_in, w_out, final_g.reshape(1, d))


def _conv_norm_swish(aext_ref, ash_ref, cw_ref, cb, lg, lb, r0, rows):
    span = rows + 2 * CONV_HALO - SUBLANES
    for r in range(1, SUBLANES):
        ash_ref[r - 1, 0:span, :] = aext_ref[r0 + r:r0 + r + span, :]
    acc = jnp.zeros((rows, D_A), F32) + cb
    off = CONV_HALO - CONV_W // 2
    for j in range(CONV_W):
        r, base = (off + j) % SUBLANES, (off + j) // SUBLANES * SUBLANES
        tap = aext_ref[r0 + base:r0 + base + rows, :] if r == 0 else ash_ref[r - 1, base:base + rows, :]
        acc = acc + cw_ref[j:j + 1, :] * tap
    mu = jnp.mean(acc, axis=-1, keepdims=True)
    cen = acc - mu
    var = jnp.mean(cen * cen, axis=-1, keepdims=True)
    return _silu(cen * lax.rsqrt(var + EPS) * lg + lb)


def _ffn_fused_kernel(*refs, pre, post, halves):
    refs = list(refs)
    take = lambda n: [refs.pop(0) for _ in range(n)]
    x_ref, mod_ref, g_ref, win_ref, wout_ref = take(5)
    if pre == "mix":
        ap_ref, ac_ref, an_ref, zt_ref, gmix_ref, woa_ref, woz_ref, cw_ref, cb_ref, lg_ref, lb_ref = take(11)
    if post == "proj":
        pmod_ref, pg_ref, wa_ref, wbt_ref = take(4)
        o_ref, a_ref, bpt_ref = take(3)
    elif post == "qkv":
        pmod_ref, pg_ref, wqkv_ref = take(3)
        o_ref, q_ref, k_ref, v_ref = take(4)
    else:
        (o_ref,) = take(1)
    if pre == "mix":
        aext_ref, ash_ref = take(2)
    assert not refs

    tm = x_ref.shape[0]
    rows = tm // halves
    i = pl.program_id(1)
    if pre == "mix":
        aext_ref[0:CONV_HALO, :] = jnp.where(i > 0, ap_ref[...], 0.0)
        aext_ref[CONV_HALO:CONV_HALO + tm, :] = ac_ref[...]
        aext_ref[CONV_HALO + tm:, :] = jnp.where(i < pl.num_programs(1) - 1, an_ref[...], 0.0)

    def start(hf):
        rs = slice(hf * rows, (hf + 1) * rows)
        x = x_ref[rs, :]
        if pre == "mix":
            a = _conv_norm_swish(aext_ref, ash_ref, cw_ref, cb_ref[...], lg_ref[...], lb_ref[...], hf * rows, rows)
            y = jnp.dot(a.astype(BF16), woa_ref[...], preferred_element_type=F32)
            y = y + lax.dot_general(zt_ref[:, rs], woz_ref[...], (((0,), (0,)), ((), ())),
                                    preferred_element_type=F32)
            x = x + gmix_ref[...] * y
        h = _norm_mod(x, g_ref[...], mod_ref[0:1, :], mod_ref[1:2, :]).astype(BF16)
        gate = jnp.dot(h, win_ref[:, :D_FF], preferred_element_type=F32)
        up = jnp.dot(h, win_ref[:, D_FF:], preferred_element_type=F32)
        return rs, x, gate, up

    def finish(st):
        rs, x, gate, up = st
        act = (_silu(gate) * up).astype(BF16)
        y = x + (0.5 * mod_ref[2:3, :]) * jnp.dot(act, wout_ref[...], preferred_element_type=F32)
        o_ref[rs, :] = y
        return rs, y

    def epilogue(done):
        rs, y = done
        if post is None:
            return
        h2 = _norm_mod(y, pg_ref[...], pmod_ref[0:1, :], pmod_ref[1:2, :]).astype(BF16)
        if post == "proj":
            pa = jnp.dot(h2, wa_ref[...], preferred_element_type=F32)
            a_ref[rs, :] = pa[:, :D_A] * jax.nn.sigmoid(pa[:, D_A:])
            bpt_ref[:, rs] = lax.dot_general(wbt_ref[...], h2, (((1,), (1,)), ((), ())),
                                             preferred_element_type=F32)
        else:
            qkv = jnp.dot(h2, wqkv_ref[...], preferred_element_type=F32)
            nq, nk = N_HEADS * HEAD_DIM, N_KV * HEAD_DIM
            q_ref[rs, :] = (qkv[:, :nq] * (HEAD_DIM ** -0.5 * LOG2E)).astype(BF16)

            def with_swapped(t):
                sw = [pltpu.roll(t[:, j * LANES:(j + 1) * LANES], HEAD_DIM, axis=1) for j in range(nk // LANES)]
                return jnp.concatenate([t] + sw, axis=1).astype(BF16)

            k_ref[rs, :] = with_swapped(qkv[:, nq:nq + nk])
            v_ref[rs, :] = with_swapped(qkv[:, nq + nk:])

    st = start(0)
    for hf in range(halves):
        done = finish(st)
        if hf + 1 < halves:
            st = start(hf + 1)
        epilogue(done)


def _ffn_fused(x, mod3, norm_g, w_in, w_out, *, pre=None, post=None, tm=512, halves=2):
    b, l, d = x.shape
    tile = lambda w: pl.BlockSpec((None, tm, w), lambda bi, i: (bi, i, 0))
    per_seq = lambda r: pl.BlockSpec((None, r, d), lambda bi, i: (bi, 0, 0))
    args = [x, mod3, norm_g.reshape(1, d), w_in, w_out]
    specs = [tile(d), per_seq(3), _const_spec((1, d)), _const_spec(w_in.shape), _const_spec(w_out.shape)]
    scratch = []
    pre_kind = post_kind = None
    if pre is not None:
        pre_kind, a_glu, zt, gate_mix, woa, woz, conv_w, conv_b, ln_g, ln_b = pre
        nh, nblk = tm // CONV_HALO, l // CONV_HALO
        args += [a_glu, a_glu, a_glu, zt, gate_mix, woa, woz, conv_w, conv_b.reshape(1, D_A), ln_g.reshape(1, D_A),
                 ln_b.reshape(1, D_A)]
        specs += [pl.BlockSpec((None, CONV_HALO, D_A), lambda bi, i: (bi, jnp.maximum(i * nh - 1, 0), 0)),
                  tile(D_A),
                  pl.BlockSpec((None, CONV_HALO, D_A), lambda bi, i: (bi, jnp.minimum((i + 1) * nh, nblk - 1), 0)),
                  pl.BlockSpec((None, D_B, tm), lambda bi, i: (bi, 0, i)),
                  per_seq(1), _const_spec(woa.shape), _const_spec(woz.shape), _const_spec(conv_w.shape),
                  _const_spec((1, D_A)), _const_spec((1, D_A)), _const_spec((1, D_A))]
        rows = tm // halves
        scratch = [pltpu.VMEM((tm + 2 * CONV_HALO, D_A), F32),
                   pltpu.VMEM((SUBLANES - 1, rows + 2 * CONV_HALO - SUBLANES, D_A), F32)]
    out_specs = [tile(d)]
    out_shape = [jax.ShapeDtypeStruct(x.shape, F32)]
    if post is not None:
        post_kind = post[0]
        args += [post[1], post[2].reshape(1, d)] + list(post[3:])
        specs += [per_seq(3), _const_spec((1, d))] + [_const_spec(w.shape) for w in post[3:]]
        if post_kind == "proj":
            out_specs += [tile(D_A), pl.BlockSpec((None, 3 * D_B, tm), lambda bi, i: (bi, 0, i))]
            out_shape += [jax.ShapeDtypeStruct((b, l, D_A), F32), jax.ShapeDtypeStruct((b, 3 * D_B, l), F32)]
        else:
            nq, nk = N_HEADS * HEAD_DIM, N_KV * HEAD_DIM
            out_specs += [tile(nq), tile(2 * nk), tile(2 * nk)]
            out_shape += [jax.ShapeDtypeStruct((b, l, nq), BF16), jax.ShapeDtypeStruct((b, l, 2 * nk), BF16),
                          jax.ShapeDtypeStruct((b, l, 2 * nk), BF16)]
    kern = functools.partial(_ffn_fused_kernel, pre=pre_kind, post=post_kind, halves=halves)
    return pl.pallas_call(
        kern,
        grid=(b, l // tm),
        in_specs=specs,
        out_specs=out_specs,
        out_shape=out_shape,
        scratch_shapes=scratch,
        compiler_params=_params(("arbitrary", "arbitrary")),
        name="ffn_" + "_".join(k for k in (pre_kind, post_kind) if k),
    )(*args)


def _fft_tables(l):
    n = 2 * l
    n1 = n // LANES
    h = n1 // 2
    nr = min(n1, -(-(h + 1) // BF16_ROWS) * BF16_ROWS)
    k1 = np.arange(nr)[:, None].astype(np.float64)
    r1 = np.arange(n1)[None, :].astype(np.float64)
    ang1 = 2.0 * np.pi * k1 * r1 / n1
    f1 = np.concatenate([np.cos(ang1), -np.sin(ang1)], axis=0)
    n2 = np.arange(LANES)[None, :].astype(np.float64)
    angt = 2.0 * np.pi * k1 * n2 / n
    tc = np.concatenate([np.cos(angt), np.cos(angt)], axis=1)
    ts = np.concatenate([np.sin(angt), -np.sin(angt)], axis=1)
    a2 = 2.0 * np.pi * np.arange(LANES)[:, None] * np.arange(LANES)[None, :] / LANES
    c2, s2 = np.cos(a2), -np.sin(a2)
    f2f = np.block([[c2, s2], [-s2, c2]])
    f2i = np.block([[c2, -s2], [s2, c2]])
    ang4 = 2.0 * np.pi * np.arange(h)[:, None] * np.arange(h)[None, :] / n1
    wgt = np.where(np.arange(h)[None, :] == 0, 1.0, 2.0)
    g4 = np.concatenate([wgt * np.cos(ang4), -wgt * np.sin(ang4)], axis=1)
    sgn = np.broadcast_to(((-1.0) ** np.arange(h))[:, None], (h, 2 * LANES))
    f32 = lambda a: jnp.asarray(a, F32)
    return dict(f1=f32(f1), f1h=f32(f1[:, :n1 // 2]), tc=f32(tc), ts=f32(ts), f2f=f32(f2f), f2i=f32(f2i),
                g4=f32(g4), sgn=f32(sgn))


def _swap(a):
    return jnp.concatenate([a[..., LANES:], a[..., :LANES]], axis=-1)


def _stage1_pair(f1, z0, z1, tc, ts):
    zp = jnp.concatenate([z0, z1], axis=1).astype(BF16)
    aa = jnp.dot(f1, zp, preferred_element_type=F32)
    nr = aa.shape[0] // 2
    outs = []
    for q in range(2):
        a = jnp.concatenate([aa[:nr, q * LANES:(q + 1) * LANES], aa[nr:, q * LANES:(q + 1) * LANES]], axis=1)
        a = a.astype(BF16)
        outs.append(a * tc + _swap(a) * ts)
    return outs


def _filt_gen_kernel(w1t_ref, b1_ref, w2t_ref, b2_ref, fr_ref, w3t_ref, dl_ref, k_ref, ssq_ref, *, l, tt):
    i = pl.program_id(0)
    inv_lm1 = 1.0 / (l - 1)
    n = i * tt + lax.broadcasted_iota(jnp.int32, (1, tt), 1)
    j = jnp.where(n < l, n, 2 * l - n).astype(F32)
    t = j * inv_lm1
    w = (2.0 * math.pi) * j / l
    bands = 1e-4 + lax.broadcasted_iota(jnp.int32, (HY_BANDS, 1), 0).astype(F32) * ((HY_BANDS - 1 - 1e-4) / (HY_BANDS - 1))
    fw = bands * w
    fr = fr_ref[...]
    z1 = (w1t_ref[:, 0:1] * t
          + jnp.dot(w1t_ref[:, 1:1 + HY_BANDS], jnp.cos(fw), precision=HI, preferred_element_type=F32)
          - jnp.dot(w1t_ref[:, 1 + HY_BANDS:], jnp.sin(fw), precision=HI, preferred_element_type=F32)
          + b1_ref[...])
    h1 = jnp.sin(fr * z1)
    h2 = jnp.sin(fr * (jnp.dot(w2t_ref[...], h1, precision=HI, preferred_element_type=F32) + b2_ref[...]))
    ht = jnp.dot(w3t_ref[...], h2.astype(BF16), preferred_element_type=F32)
    dec = jnp.exp(-t * dl_ref[...])
    dec = jnp.where(n == l, 0.0, dec)
    k = ht * jnp.concatenate([dec] * HY_ORDER, axis=0)
    k_ref[...] = k.astype(k_ref.dtype)

    @pl.when(i == 0)
    def _():
        ssq_ref[...] = jnp.zeros_like(ssq_ref)

    k2 = k * k
    part = k2[:, 0:LANES]
    for q in range(1, tt // LANES):
        part = part + k2[:, q * LANES:(q + 1) * LANES]
    ssq_ref[...] += part


def _filt_gen(hw1, hb1, hw2, hb2, hw3, hfreq, l, *, tt=1024):
    n = 2 * l
    nt = n // tt
    rows = HY_ORDER * D_B
    w3t = hw3.T.reshape(2, rows, HY_HIDDEN).astype(BF16)
    max_decay = math.log(HY_TARGET) / HY_FAST_PCT
    min_decay = math.log(HY_TARGET) / HY_SLOW_PCT
    deltas = jnp.abs(jnp.linspace(min_decay, max_decay, D_B, dtype=F32)).reshape(D_B, 1)
    col = lambda a: a.reshape(HY_HIDDEN, 1)
    kern = functools.partial(_filt_gen_kernel, l=l, tt=tt)
    return pl.pallas_call(
        kern,
        grid=(nt,),
        in_specs=[_const_spec((HY_HIDDEN, 2 * HY_BANDS + 1)), _const_spec((HY_HIDDEN, 1)),
                  _const_spec((HY_HIDDEN, HY_HIDDEN)), _const_spec((HY_HIDDEN, 1)), _const_spec((HY_HIDDEN, 1)),
                  pl.BlockSpec((None, rows, HY_HIDDEN), lambda i: (jnp.where(i * tt >= l, 1, 0), 0, 0)),
                  _const_spec((D_B, 1))],
        out_specs=[pl.BlockSpec((rows, tt), lambda i: (0, i)),
                   pl.BlockSpec((rows, LANES), lambda i: (0, 0))],
        out_shape=[jax.ShapeDtypeStruct((rows, n), BF16), jax.ShapeDtypeStruct((rows, LANES), F32)],
        compiler_params=_params(("arbitrary",)),
        name="hyena_filter_gen",
    )(hw1.T, col(hb1), hw2.T, col(hb2), col(hfreq), w3t, deltas)


def _filt_fft_kernel(k_ref, ssq_ref, f1_ref, tc_ref, ts_ref, f2_ref, o_ref, *, ct, n_total):
    tc, ts = tc_ref[...].astype(BF16), ts_ref[...].astype(BF16)
    f1, f2 = f1_ref[...].astype(BF16), f2_ref[...].astype(BF16)
    nr = tc.shape[0]
    b = []
    for p in range(ct // 2):
        b0, b1 = _stage1_pair(f1, k_ref[2 * p], k_ref[2 * p + 1], tc, ts)
        b.append(jnp.concatenate([b0, b1], axis=0))
    x = [jnp.dot(bp, f2, preferred_element_type=F32) for bp in b]
    for p in range(ct // 2):
        for q in range(2):
            c = 2 * p + q
            tot = jnp.sum(ssq_ref[c], axis=-1, keepdims=True)
            o_ref[c] = (x[p][q * nr:(q + 1) * nr] * (lax.rsqrt(tot + EPS) * (1.0 / n_total))).astype(o_ref.dtype)


def _filt_fft(k, ssq, tabs, l, *, ct=8):
    rows = k.shape[0]
    n = 2 * l
    n1 = n // LANES
    nr = tabs["tc"].shape[0]
    kern = functools.partial(_filt_fft_kernel, ct=ct, n_total=n)
    return pl.pallas_call(
        kern,
        grid=(rows // ct,),
        in_specs=[pl.BlockSpec((ct, n1, LANES), lambda c: (c, 0, 0)),
                  pl.BlockSpec((ct, 1, LANES), lambda c: (c, 0, 0)),
                  _const_spec(tabs["f1"].shape), _const_spec(tabs["tc"].shape), _const_spec(tabs["ts"].shape),
                  _const_spec(tabs["f2f"].shape)],
        out_specs=pl.BlockSpec((ct, nr, 2 * LANES), lambda c: (c, 0, 0)),
        out_shape=jax.ShapeDtypeStruct((rows, nr, 2 * LANES), BF16),
        compiler_params=_params(("arbitrary",)),
        name="hyena_filter_fft",
    )(k.reshape(rows, n1, LANES), ssq.reshape(rows, 1, LANES), tabs["f1"], tabs["tc"], tabs["ts"], tabs["f2f"])


def _shift_prev(a, row, lane):
    nv = a.shape[0] // SUBLANES
    r3 = pltpu.roll(a, 1, axis=1).reshape(nv, SUBLANES, LANES)
    rot = pltpu.roll(r3, 1, axis=1)
    rot_prev = jnp.concatenate([jnp.zeros((1, SUBLANES, LANES), a.dtype), rot[:-1]], axis=0)
    out = jnp.where((lane == 0) & (row == 0), rot_prev, jnp.where(lane == 0, rot, r3))
    return out.reshape(a.shape)


def _shift_next(a, row, lane):
    nv = a.shape[0] // SUBLANES
    r3 = pltpu.roll(a, LANES - 1, axis=1).reshape(nv, SUBLANES, LANES)
    rot = pltpu.roll(r3, SUBLANES - 1, axis=1)
    rot_next = jnp.concatenate([rot[1:], jnp.zeros((1, SUBLANES, LANES), a.dtype)], axis=0)
    out = jnp.where((lane == LANES - 1) & (row == SUBLANES - 1), rot_next, jnp.where(lane == LANES - 1, rot, r3))
    return out.reshape(a.shape)


def _hyena_kernel(sw_ref, sb_ref, skip_ref, bp_ref, kf_ref, f1_ref, tc_ref, ts_ref, f2f_ref, f2i_ref, g4_ref,
                  sgn_ref, o_ref, *, ct):
    ci = pl.program_id(1)
    nh = bp_ref.shape[2]
    nr = tc_ref.shape[0]
    npair = ct // 2
    row = lax.broadcasted_iota(jnp.int32, (nh // SUBLANES, SUBLANES, LANES), 1)
    lane = lax.broadcasted_iota(jnp.int32, (nh // SUBLANES, SUBLANES, LANES), 2)
    tc, ts = tc_ref[...].astype(BF16), ts_ref[...].astype(BF16)
    f1, f2f, f2i, g4 = (r[...].astype(BF16) for r in (f1_ref, f2f_ref, f2i_ref, g4_ref))

    def short_conv(part, c):
        ch = part * D_B + ci * ct + c
        a = bp_ref[part, c]
        return (sw_ref[0, ch] * _shift_prev(a, row, lane) + sw_ref[1, ch] * a
                + sw_ref[2, ch] * _shift_next(a, row, lane) + sb_ref[ch])

    def stage1(z0, z1):
        b0, b1 = _stage1_pair(f1, z0, z1, tc, ts)
        return jnp.concatenate([b0, b1], axis=0)

    def stage2(b):
        return jnp.dot(b, f2f, preferred_element_type=F32)

    def filt(x, order, p):
        outs = []
        for q in range(2):
            kf = kf_ref[order, 2 * p + q]
            kr, ki = kf[:, :LANES], kf[:, LANES:]
            xq = x[q * nr:(q + 1) * nr].astype(BF16)
            outs.append(xq * jnp.concatenate([kr, kr], axis=-1) + _swap(xq) * jnp.concatenate([-ki, ki], axis=-1))
        return jnp.concatenate(outs, axis=0)

    def stage3(y):
        return jnp.dot(y, f2i, preferred_element_type=F32)

    def twiddle_inv(cm):
        cm = cm.astype(BF16)
        d0 = cm[:nr] * tc - _swap(cm[:nr]) * ts
        d1 = cm[nr:] * tc - _swap(cm[nr:]) * ts
        rhs = jnp.concatenate(
            [jnp.concatenate([d0[:nh, :LANES], d1[:nh, :LANES]], axis=1),
             jnp.concatenate([d0[:nh, LANES:], d1[:nh, LANES:]], axis=1)], axis=0)
        mid = jnp.concatenate([d0[nh:nh + 1, :LANES], d1[nh:nh + 1, :LANES]], axis=1).astype(F32)
        return rhs, mid

    def stage4(d):
        rhs, mid = d
        return jnp.dot(g4, rhs, preferred_element_type=F32) + sgn_ref[...] * mid

    def long_conv(zs, order):
        b = [stage1(*zs[p]) for p in range(npair)]
        x = [stage2(b[p]) for p in range(npair)]
        y = [filt(x[p], order, p) for p in range(npair)]
        c = [stage3(y[p]) for p in range(npair)]
        d = [twiddle_inv(c[p]) for p in range(npair)]
        o = [stage4(d[p]) for p in range(npair)]
        return [(o[p][:, :LANES], o[p][:, LANES:]) for p in range(npair)]

    v = [(short_conv(0, 2 * p), short_conv(0, 2 * p + 1)) for p in range(npair)]
    y1 = long_conv(v, 0)
    z1 = []
    for p in range(npair):
        pr = []
        for q in range(2):
            c = 2 * p + q
            pr.append(short_conv(1, c) * (y1[p][q] + v[p][q] * skip_ref[0, ci * ct + c]))
        z1.append(tuple(pr))
    y2 = long_conv(z1, 1)
    for p in range(npair):
        for q in range(2):
            c = 2 * p + q
            o_ref[c] = (short_conv(2, c) * (y2[p][q] + z1[p][q] * skip_ref[1, ci * ct + c])).astype(o_ref.dtype)


def _hyena(bpt, kf, short_w, short_b, skip, tabs, *, ct=8):
    b, _, l = bpt.shape
    nh = l // LANES
    nr = tabs["tc"].shape[0]
    smem = pl.BlockSpec(memory_space=pltpu.SMEM)
    kern = functools.partial(_hyena_kernel, ct=ct)
    out = pl.pallas_call(
        kern,
        grid=(b, D_B // ct),
        in_specs=[smem, smem, smem,
                  pl.BlockSpec((None, 3, ct, nh, LANES), lambda bi, c: (bi, 0, c, 0, 0)),
                  pl.BlockSpec((HY_ORDER, ct, nr, 2 * LANES), lambda bi, c: (0, c, 0, 0)),
                  _const_spec(tabs["f1h"].shape), _const_spec(tabs["tc"].shape), _const_spec(tabs["ts"].shape),
                  _const_spec(tabs["f2f"].shape), _const_spec(tabs["f2i"].shape), _const_spec(tabs["g4"].shape),
                  _const_spec(tabs["sgn"].shape)],
        out_specs=pl.BlockSpec((None, ct, nh, LANES), lambda bi, c: (bi, c, 0, 0)),
        out_shape=jax.ShapeDtypeStruct((b, D_B, nh, LANES), BF16),
        compiler_params=_params(("arbitrary", "arbitrary")),
        name="hyena_conv",
    )(short_w, short_b, skip, bpt.reshape(b, 3, D_B, nh, LANES), kf.reshape(HY_ORDER, D_B, nr, 2 * LANES),
      tabs["f1h"], tabs["tc"], tabs["ts"], tabs["f2f"], tabs["f2i"], tabs["g4"], tabs["sgn"])
    return out.reshape(b, D_B, l)


def _alibi_slopes():
    return [2.0 ** (-8.0 * (i + 1) / N_HEADS) for i in range(N_HEADS)]


def _attn_kernel(sink_ref, b0_ref, b1_ref, b2_ref, x_ref, q_ref, kp_ref, kc_ref, kn_ref, vp_ref, vc_ref, vn_ref,
                 mod_ref, wo_ref, o_ref, kall_ref, vall_ref, oh_ref, *, tq, ahead):
    i = pl.program_id(1)
    nk = N_KV * HEAD_DIM
    kall_ref[0:WINDOW, :] = kp_ref[...]
    kall_ref[WINDOW:WINDOW + tq, :] = kc_ref[...]
    kall_ref[WINDOW + tq:, :] = kn_ref[...]
    vall_ref[0:WINDOW, :] = vp_ref[...]
    vall_ref[WINDOW:WINDOW + tq, :] = vc_ref[...]
    vall_ref[WINDOW + tq:, :] = vn_ref[...]

    span = 3 * WINDOW
    lane = lax.broadcasted_iota(jnp.int32, (WINDOW, LANES), 1)
    low = lane < HEAD_DIM

    def sub_block(sb, carry):
        r0 = pl.multiple_of(sb * WINDOW, WINDOW)
        first = (i == 0) & (sb == 0)
        lastb = (i == pl.num_programs(1) - 1) & (sb == tq // WINDOW - 1)

        def cols_for(head):
            kvh = head // GROUP
            tile = kvh // 2
            plain = slice(tile * LANES, (tile + 1) * LANES)
            swapped = slice(nk + tile * LANES, nk + (tile + 1) * LANES)
            return plain if kvh % 2 == head % 2 else swapped

        def scores(head):
            pair = head // 2
            q2 = q_ref[pl.ds(r0, WINDOW), pair * LANES:(pair + 1) * LANES]
            qz = jnp.zeros_like(q2)
            qm = jnp.where(low, q2, qz) if head % 2 == 0 else jnp.where(low, qz, q2)
            kw = kall_ref[pl.ds(r0, span), cols_for(head)]
            s = lax.dot_general(qm, kw, (((1,), (1,)), ((), ())), preferred_element_type=F32)
            bias = jnp.concatenate([b0_ref[jnp.where(first, N_HEADS, head)], b1_ref[head],
                                    b2_ref[jnp.where(lastb, N_HEADS, head)]], axis=1)
            return s + bias

        pend = [scores(h) for h in range(ahead)]
        outs = []
        for head in range(N_HEADS):
            s = pend.pop(0)
            if head + ahead < N_HEADS:
                pend.append(scores(head + ahead))
            vw = vall_ref[pl.ds(r0, span), cols_for(head)]
            snk = sink_ref[head] * LOG2E
            m = jnp.maximum(jnp.max(s, axis=-1, keepdims=True), snk)
            p = jnp.exp2(s - m)
            den = jnp.sum(p, axis=-1, keepdims=True) + jnp.exp2(snk - m)
            outs.append(jnp.dot(p.astype(BF16), vw, preferred_element_type=F32) / den)
            if head % 2 == 1:
                pair = head // 2
                oh_ref[pl.ds(r0, WINDOW), pair * LANES:(pair + 1) * LANES] = (
                    jnp.where(low, outs[0], outs[1]).astype(BF16))
                outs = []
        return carry

    lax.fori_loop(0, tq // WINDOW, sub_block, 0)
    y = jnp.dot(oh_ref[...], wo_ref[...], preferred_element_type=F32)
    o_ref[...] = x_ref[...] + mod_ref[2:3, :] * y


def _attn(x, q, kk, vv, mod3, sink, w_out, *, tq=512, ahead=5):
    b, l, d = x.shape
    nq, nk2 = q.shape[-1], kk.shape[-1]
    r = tq // WINDOW
    nblk = l // WINDOW
    prev = lambda bi, i: (bi, jnp.maximum(i * r - 1, 0), 0)
    cur = lambda bi, i: (bi, i, 0)
    nxt = lambda bi, i: (bi, jnp.minimum((i + 1) * r, nblk - 1), 0)
    dist = np.abs(np.arange(3 * WINDOW)[None, :] - WINDOW - np.arange(WINDOW)[:, None])
    bias = -np.asarray(_alibi_slopes())[:, None, None] * dist[None] * LOG2E
    bias = np.where(dist[None] <= WINDOW, bias, -np.inf)
    gone = np.full((1, WINDOW, WINDOW), -np.inf)
    b0 = jnp.asarray(np.concatenate([bias[:, :, :WINDOW], gone]), F32)
    b1 = jnp.asarray(bias[:, :, WINDOW:2 * WINDOW], F32)
    b2 = jnp.asarray(np.concatenate([bias[:, :, 2 * WINDOW:], gone]), F32)
    kern = functools.partial(_attn_kernel, tq=tq, ahead=ahead)
    return pl.pallas_call(
        kern,
        grid=(b, l // tq),
        in_specs=[pl.BlockSpec(memory_space=pltpu.SMEM),
                  _const_spec(b0.shape), _const_spec(b1.shape), _const_spec(b2.shape),
                  pl.BlockSpec((None, tq, d), cur),
                  pl.BlockSpec((None, tq, nq), cur),
                  pl.BlockSpec((None, WINDOW, nk2), prev), pl.BlockSpec((None, tq, nk2), cur),
                  pl.BlockSpec((None, WINDOW, nk2), nxt),
                  pl.BlockSpec((None, WINDOW, nk2), prev), pl.BlockSpec((None, tq, nk2), cur),
                  pl.BlockSpec((None, WINDOW, nk2), nxt),
                  pl.BlockSpec((None, 3, d), lambda bi, i: (bi, 0, 0)),
                  _const_spec(w_out.shape)],
        out_specs=pl.BlockSpec((None, tq, d), cur),
        out_shape=jax.ShapeDtypeStruct(x.shape, F32),
        scratch_shapes=[pltpu.VMEM((tq + 2 * WINDOW, nk2), BF16), pltpu.VMEM((tq + 2 * WINDOW, nk2), BF16),
                        pltpu.VMEM((tq, nq), BF16)],
        compiler_params=_params(("arbitrary", "arbitrary")),
        name="attn",
    )(sink, b0, b1, b2, x, q, kk, kk, kk, vv, vv, vv, mod3, w_out)


def _trunk(x, mods, w, filt, tabs):
    m = [mods[0][:, 3 * s:3 * s + 3] for s in range(N_SUB)]
    x, a_glu, bpt = _ffn_fused(x, m[0], w["norm_g"][0, 0], w["ffn_in"][0][0], w["ffn_out"][0][0],
                               post=("proj", m[1], w["norm_g"][0, 1], w["ab_wa"], w["ab_wbt"]))
    zt = _hyena(bpt, filt, w["hy_short_w"], w["hy_short_b"], w["hy_skip"], tabs)
    x = _ffn_fused(x, m[2], w["norm_g"][0, 2], w["ffn_in"][0][1], w["ffn_out"][0][1],
                   pre=("mix", a_glu, zt, m[1][:, 2:3], w["ab_woa"], w["ab_woz"], w["conv_w"], w["conv_b"],
                        w["conv_ln_g"], w["conv_ln_b"]), halves=1)[0]

    m = [mods[1][:, 3 * s:3 * s + 3] for s in range(N_SUB)]
    x, q, kk, vv = _ffn_fused(x, m[0], w["norm_g"][1, 0], w["ffn_in"][1][0], w["ffn_out"][1][0],
                              post=("qkv", m[1], w["norm_g"][1, 1], w["attn_w_qkv"]))
    x = _attn(x, q, kk, vv, m[1], w["attn_sink"], w["attn_w_out"])
    x = _ffn(x, m[2], w["norm_g"][1, 2], w["ffn_in"][1][1], w["ffn_out"][1][1], w["final_g"], final=True)
    return x


def kernel(x_prompt, x_sample, c_prompt, c_sample, ada_w, ada_b, norm_g, ffn_w_in, ffn_w_out, final_g, ab_w_in,
           conv_w, conv_b, conv_ln_g, conv_ln_b, hy_short_w, hy_short_b, hy_w1, hy_b1, hy_w2, hy_b2, hy_w3,
           hy_freq, hy_skip, ab_w_out, attn_w_qkv, attn_sink, attn_w_out):
    l = x_prompt.shape[1]
    assert x_sample.shape[1] == l and l % 512 == 0
    bp, bs = x_prompt.shape[0], x_sample.shape[0]
    c_pad = jnp.zeros((8, D_MODEL), F32).at[:bp].set(c_prompt).at[bp:bp + bs].set(c_sample)
    mod = _ada_mod(c_pad, ada_w, ada_b)
    depth = mod.shape[0]
    mod = mod.reshape(depth, 8, 3 * N_SUB, D_MODEL)

    w_in16, w_out16 = _to_bf16(ffn_w_in), _to_bf16(ffn_w_out)
    wa, wbt, woa, woz, wqkv, wo = _mixer_weights(ab_w_in, ab_w_out, attn_w_qkv, attn_w_out)
    w = dict(
        norm_g=norm_g, final_g=final_g,
        ffn_in=[[w_in16[i, s] for s in range(2)] for i in range(depth)],
        ffn_out=[[w_out16[i, s] for s in range(2)] for i in range(depth)],
        ab_wa=wa, ab_wbt=wbt,
        conv_w=conv_w[0], conv_b=conv_b[0], conv_ln_g=conv_ln_g[0], conv_ln_b=conv_ln_b[0],
        hy_short_w=hy_short_w[0], hy_short_b=hy_short_b[0], hy_skip=hy_skip[0],
        ab_woa=woa, ab_woz=woz, attn_w_qkv=wqkv, attn_sink=attn_sink[0], attn_w_out=wo,
    )
    tabs = _fft_tables(l)
    kt, ssq = _filt_gen(hy_w1[0], hy_b1[0], hy_w2[0], hy_b2[0], hy_w3[0], hy_freq[0], l)
    filt = _filt_fft(kt, ssq, tabs, l)

    y_prompt = _trunk(x_prompt, mod[:, :bp], w, filt, tabs)
    y_sample = _trunk(x_sample, mod[:, bp:bp + bs], w, filt, tabs)
    return (y_prompt, y_sample)
```

```python
import functools
import math

import numpy as np
import jax
import jax.numpy as jnp
from jax import lax
from jax.experimental import pallas as pl
from jax.experimental.pallas import tpu as pltpu

F32 = jnp.float32
BF16 = jnp.bfloat16

D_MODEL = 1024
D_FF = 2816
N_SUB = 3
EPS = 1e-6
D_A = 512
CONV_W = 31
CONV_HALO = 16
D_B = 512
HY_ORDER = 2
HY_BANDS = 16
HY_HIDDEN = 64
HY_FAST_PCT = 0.3
HY_SLOW_PCT = 1.5
HY_TARGET = 1e-2
N_HEADS = 16
N_KV = 4
GROUP = 4
HEAD_DIM = 64
WINDOW = 128
LOG2E = math.log2(math.e)

SUBLANES = 8
BF16_ROWS = 16
LANES = 128
VMEM_LIMIT = 56 * 1024 * 1024
HI = lax.Precision.HIGHEST


def _params(sem, vmem=VMEM_LIMIT):
    return pltpu.CompilerParams(dimension_semantics=sem, vmem_limit_bytes=vmem)


def _const_spec(shape):
    nd = len(shape)
    return pl.BlockSpec(shape, lambda *_: (0,) * nd, pipeline_mode=pl.Buffered(1))


def _stacked_spec(stacked):
    w, idx = stacked
    k = len(idx)
    return pl.BlockSpec((None,) * k + w.shape[k:], lambda *_: tuple(idx) + (0,) * (w.ndim - k),
                        pipeline_mode=pl.Buffered(1))


def _silu(x):
    return x * jax.nn.sigmoid(x)


def _norm_mod(x, g, shift, scale):
    ms = jnp.mean(x * x, axis=-1, keepdims=True)
    return (x * lax.rsqrt(ms + EPS) * g) * (1.0 + scale) + shift


def _ada_kernel(c_ref, w_ref, b_ref, o_ref):
    cs = _silu(c_ref[...])
    o_ref[...] = jnp.dot(cs, w_ref[...], precision=HI, preferred_element_type=F32) + b_ref[...]


def _ada_mod(c_pad, ada_w, ada_b):
    depth, d, n = ada_w.shape
    tn = 1152
    return pl.pallas_call(
        _ada_kernel,
        grid=(depth, n // tn),
        in_specs=[pl.BlockSpec((8, d), lambda l, j: (0, 0)),
                  pl.BlockSpec((None, d, tn), lambda l, j: (l, 0, j)),
                  pl.BlockSpec((None, 1, tn), lambda l, j: (l, 0, j))],
        out_specs=pl.BlockSpec((None, 8, tn), lambda l, j: (l, 0, j)),
        out_shape=jax.ShapeDtypeStruct((depth, 8, n), F32),
        compiler_params=_params(("arbitrary", "arbitrary")),
        name="ada_mod",
    )(c_pad, ada_w, ada_b.reshape(depth, 1, n))


def _cast_kernel(x_ref, o_ref):
    o_ref[...] = x_ref[...].astype(o_ref.dtype)


def _to_bf16(w, *, block_bytes=8 * 1024 * 1024):
    shape = w.shape
    c = shape[-1]
    w2 = w.reshape(-1, c)
    r = w2.shape[0]
    tr = r
    while tr * c * 4 > block_bytes and tr % (2 * BF16_ROWS) == 0:
        tr //= 2
    out = pl.pallas_call(
        _cast_kernel,
        grid=(r // tr,),
        in_specs=[pl.BlockSpec((tr, c), lambda i: (i, 0))],
        out_specs=pl.BlockSpec((tr, c), lambda i: (i, 0)),
        out_shape=jax.ShapeDtypeStruct((r, c), BF16),
        compiler_params=_params(("arbitrary",)),
        name="cast_bf16",
    )(w2)
    return out.reshape(shape)


def _mixer_weights_kernel(abin_ref, about_ref, qkv_ref, wo_ref, wa_ref, wbt_ref, woa_ref, woz_ref, wqkv_ref, wo16_ref):
    wa_ref[...] = abin_ref[:, :2 * D_A].astype(BF16)
    wbt_ref[...] = abin_ref[:, 2 * D_A:].T.astype(BF16)
    woa_ref[...] = about_ref[:D_A, :].astype(BF16)
    woz_ref[...] = about_ref[D_A:, :].astype(BF16)
    wqkv_ref[...] = qkv_ref[...].astype(BF16)
    wo16_ref[...] = wo_ref[...].astype(BF16)


def _mixer_weights(ab_w_in, ab_w_out, attn_w_qkv, attn_w_out):
    first = lambda w: pl.BlockSpec((None,) + w.shape[1:], lambda i: (0,) * w.ndim, pipeline_mode=pl.Buffered(1))
    d = ab_w_in.shape[1]
    nqkv = attn_w_qkv.shape[-1]
    shapes = [(d, 2 * D_A), (3 * D_B, d), (D_A, d), (D_B, d), (d, nqkv), attn_w_out.shape[1:]]
    return pl.pallas_call(
        _mixer_weights_kernel,
        grid=(1,),
        in_specs=[first(ab_w_in), first(ab_w_out), first(attn_w_qkv), first(attn_w_out)],
        out_specs=[_const_spec(s) for s in shapes],
        out_shape=[jax.ShapeDtypeStruct(s, BF16) for s in shapes],
        compiler_params=_params(("arbitrary",)),
        name="mixer_weights",
    )(ab_w_in, ab_w_out, attn_w_qkv, attn_w_out)


def _ffn_kernel(x_ref, mod_ref, g_ref, win_ref, wout_ref, fg_ref, o_ref, *, final, halves):
    rows = x_ref.shape[0] // halves
    for hf in range(halves):
        rs = slice(hf * rows, (hf + 1) * rows)
        x = x_ref[rs, :]
        h = _norm_mod(x, g_ref[...], mod_ref[0:1, :], mod_ref[1:2, :]).astype(BF16)
        gate = jnp.dot(h, win_ref[:, :D_FF], preferred_element_type=F32)
        up = jnp.dot(h, win_ref[:, D_FF:], preferred_element_type=F32)
        act = (_silu(gate) * up).astype(BF16)
        y = x + (0.5 * mod_ref[2:3, :]) * jnp.dot(act, wout_ref[...], preferred_element_type=F32)
        if final:
            y = y * lax.rsqrt(jnp.mean(y * y, axis=-1, keepdims=True) + EPS) * fg_ref[...]
        o_ref[rs, :] = y


def _ffn(x, mod3, norm_g, w_in, w_out, final_g, *, final, tm=1024, halves=2):
    b, l, d = x.shape
    kern = functools.partial(_ffn_kernel, final=final, halves=halves)
    return pl.pallas_call(
        kern,
        grid=(b, l // tm),
        in_specs=[pl.BlockSpec((None, tm, d), lambda bi, i: (bi, i, 0)),
                  pl.BlockSpec((None, 3, d), lambda bi, i: (bi, 0, 0)),
                  _const_spec((1, d)),
                  _stacked_spec(w_in),
                  _stacked_spec(w_out),
                  _const_spec((1, d))],
        out_specs=pl.BlockSpec((None, tm, d), lambda bi, i: (bi, i, 0)),
        out_shape=jax.ShapeDtypeStruct(x.shape, F32),
        compiler_params=_params(("arbitrary", "arbitrary")),
        name="ffn",
    )(x, mod3, norm_g.reshape(1, d), w_in[0], w_out[0], final_g.reshape(1, d))


def _conv_norm_swish(aext_ref, ash_ref, cw_ref, cb, lg, lb, r0, rows):
    span = rows + 2 * CONV_HALO - SUBLANES
    for r in range(1, SUBLANES):
        ash_ref[r - 1, 0:span, :] = aext_ref[r0 + r:r0 + r + span, :]
    acc = jnp.zeros((rows, D_A), F32) + cb
    off = CONV_HALO - CONV_W // 2
    for j in range(CONV_W):
        r, base = (off + j) % SUBLANES, (off + j) // SUBLANES * SUBLANES
        tap = aext_ref[r0 + base:r0 + base + rows, :] if r == 0 else ash_ref[r - 1, base:base + rows, :]
        acc = acc + cw_ref[j:j + 1, :] * tap
    mu = jnp.mean(acc, axis=-1, keepdims=True)
    cen = acc - mu
    var = jnp.mean(cen * cen, axis=-1, keepdims=True)
    return _silu(cen * lax.rsqrt(var + EPS) * lg + lb)


def _ffn_fused_kernel(*refs, pre, post, halves):
    refs = list(refs)
    take = lambda n: [refs.pop(0) for _ in range(n)]
    x_ref, mod_ref, g_ref, win_ref, wout_ref = take(5)
    if pre == "mix":
        ap_ref, ac_ref, an_ref, zt_ref, gmix_ref, woa_ref, woz_ref, cw_ref, cb_ref, lg_ref, lb_ref = take(11)
    if post == "proj":
        pmod_ref, pg_ref, wa_ref, wbt_ref = take(4)
        o_ref, a_ref, bpt_ref = take(3)
    elif post == "qkv":
        pmod_ref, pg_ref, wqkv_ref = take(3)
        o_ref, q_ref, k_ref, v_ref = take(4)
    else:
        (o_ref,) = take(1)
    if pre == "mix":
        aext_ref, ash_ref = take(2)
    assert not refs

    tm = x_ref.shape[0]
    rows = tm // halves
    i = pl.program_id(1)
    if pre == "mix":
        aext_ref[0:CONV_HALO, :] = jnp.where(i > 0, ap_ref[...], 0.0)
        aext_ref[CONV_HALO:CONV_HALO + tm, :] = ac_ref[...]
        aext_ref[CONV_HALO + tm:, :] = jnp.where(i < pl.num_programs(1) - 1, an_ref[...], 0.0)

    def start(hf):
        rs = slice(hf * rows, (hf + 1) * rows)
        x = x_ref[rs, :]
        if pre == "mix":
            a = _conv_norm_swish(aext_ref, ash_ref, cw_ref, cb_ref[...], lg_ref[...], lb_ref[...], hf * rows, rows)
            y = jnp.dot(a.astype(BF16), woa_ref[...], preferred_element_type=F32)
            y = y + lax.dot_general(zt_ref[:, rs], woz_ref[...], (((0,), (0,)), ((), ())),
                                    preferred_element_type=F32)
            x = x + gmix_ref[...] * y
        h = _norm_mod(x, g_ref[...], mod_ref[0:1, :], mod_ref[1:2, :]).astype(BF16)
        gate = jnp.dot(h, win_ref[:, :D_FF], preferred_element_type=F32)
        up = jnp.dot(h, win_ref[:, D_FF:], preferred_element_type=F32)
        return rs, x, gate, up

    def finish(st):
        rs, x, gate, up = st
        act = (_silu(gate) * up).astype(BF16)
        y = x + (0.5 * mod_ref[2:3, :]) * jnp.dot(act, wout_ref[...], preferred_element_type=F32)
        o_ref[rs, :] = y
        return rs, y

    def epilogue(done):
        rs, y = done
        if post is None:
            return
        h2 = _norm_mod(y, pg_ref[...], pmod_ref[0:1, :], pmod_ref[1:2, :]).astype(BF16)
        if post == "proj":
            pa = jnp.dot(h2, wa_ref[...], preferred_element_type=F32)
            a_ref[rs, :] = pa[:, :D_A] * jax.nn.sigmoid(pa[:, D_A:])
            bpt_ref[:, rs] = lax.dot_general(wbt_ref[...], h2, (((1,), (1,)), ((), ())),
                                             preferred_element_type=F32)
        else:
            qkv = jnp.dot(h2, wqkv_ref[...], preferred_element_type=F32)
            nq, nk = N_HEADS * HEAD_DIM, N_KV * HEAD_DIM
            q_ref[rs, :] = (qkv[:, :nq] * (HEAD_DIM ** -0.5 * LOG2E)).astype(BF16)

            def with_swapped(t):
                sw = [pltpu.roll(t[:, j * LANES:(j + 1) * LANES], HEAD_DIM, axis=1) for j in range(nk // LANES)]
                return jnp.concatenate([t] + sw, axis=1).astype(BF16)

            k_ref[rs, :] = with_swapped(qkv[:, nq:nq + nk])
            v_ref[rs, :] = with_swapped(qkv[:, nq + nk:])

    st = start(0)
    for hf in range(halves):
        done = finish(st)
        if hf + 1 < halves:
            st = start(hf + 1)
        epilogue(done)


def _ffn_fused(x, mod3, norm_g, w_in, w_out, *, pre=None, post=None, tm=512, halves=2):
    b, l, d = x.shape
    tile = lambda w: pl.BlockSpec((None, tm, w), lambda bi, i: (bi, i, 0))
    per_seq = lambda r: pl.BlockSpec((None, r, d), lambda bi, i: (bi, 0, 0))
    args = [x, mod3, norm_g.reshape(1, d), w_in[0], w_out[0]]
    specs = [tile(d), per_seq(3), _const_spec((1, d)), _stacked_spec(w_in), _stacked_spec(w_out)]
    scratch = []
    pre_kind = post_kind = None
    if pre is not None:
        pre_kind, a_glu, zt, gate_mix, woa, woz, conv_w, conv_b, ln_g, ln_b = pre
        nh, nblk = tm // CONV_HALO, l // CONV_HALO
        args += [a_glu, a_glu, a_glu, zt, gate_mix, woa, woz, conv_w, conv_b.reshape(1, D_A), ln_g.reshape(1, D_A),
                 ln_b.reshape(1, D_A)]
        specs += [pl.BlockSpec((None, CONV_HALO, D_A), lambda bi, i: (bi, jnp.maximum(i * nh - 1, 0), 0)),
                  tile(D_A),
                  pl.BlockSpec((None, CONV_HALO, D_A), lambda bi, i: (bi, jnp.minimum((i + 1) * nh, nblk - 1), 0)),
                  pl.BlockSpec((None, D_B, tm), lambda bi, i: (bi, 0, i)),
                  per_seq(1), _const_spec(woa.shape), _const_spec(woz.shape), _const_spec(conv_w.shape),
                  _const_spec((1, D_A)), _const_spec((1, D_A)), _const_spec((1, D_A))]
        rows = tm // halves
        scratch = [pltpu.VMEM((tm + 2 * CONV_HALO, D_A), F32),
                   pltpu.VMEM((SUBLANES - 1, rows + 2 * CONV_HALO - SUBLANES, D_A), F32)]
    out_specs = [tile(d)]
    out_shape = [jax.ShapeDtypeStruct(x.shape, F32)]
    if post is not None:
        post_kind = post[0]
        args += [post[1], post[2].reshape(1, d)] + list(post[3:])
        specs += [per_seq(3), _const_spec((1, d))] + [_const_spec(w.shape) for w in post[3:]]
        if post_kind == "proj":
            out_specs += [tile(D_A), pl.BlockSpec((None, 3 * D_B, tm), lambda bi, i: (bi, 0, i))]
            out_shape += [jax.ShapeDtypeStruct((b, l, D_A), F32), jax.ShapeDtypeStruct((b, 3 * D_B, l), F32)]
        else:
            nq, nk = N_HEADS * HEAD_DIM, N_KV * HEAD_DIM
            out_specs += [tile(nq), tile(2 * nk), tile(2 * nk)]
            out_shape += [jax.ShapeDtypeStruct((b, l, nq), BF16), jax.ShapeDtypeStruct((b, l, 2 * nk), BF16),
                          jax.ShapeDtypeStruct((b, l, 2 * nk), BF16)]
    kern = functools.partial(_ffn_fused_kernel, pre=pre_kind, post=post_kind, halves=halves)
    return pl.pallas_call(
        kern,
        grid=(b, l // tm),
        in_specs=specs,
        out_specs=out_specs,
        out_shape=out_shape,
        scratch_shapes=scratch,
        compiler_params=_params(("arbitrary", "arbitrary")),
        name="ffn_" + "_".join(k for k in (pre_kind, post_kind) if k),
    )(*args)


def _fft_tables(l):
    n = 2 * l
    n1 = n // LANES
    h = n1 // 2
    nr = min(n1, -(-(h + 1) // BF16_ROWS) * BF16_ROWS)
    k1 = np.arange(nr)[:, None].astype(np.float64)
    r1 = np.arange(n1)[None, :].astype(np.float64)
    ang1 = 2.0 * np.pi * k1 * r1 / n1
    f1 = np.concatenate([np.cos(ang1), -np.sin(ang1)], axis=0)
    n2 = np.arange(LANES)[None, :].astype(np.float64)
    angt = 2.0 * np.pi * k1 * n2 / n
    tc = np.concatenate([np.cos(angt), np.cos(angt)], axis=1)
    ts = np.concatenate([np.sin(angt), -np.sin(angt)], axis=1)
    a2 = 2.0 * np.pi * np.arange(LANES)[:, None] * np.arange(LANES)[None, :] / LANES
    c2, s2 = np.cos(a2), -np.sin(a2)
    f2f = np.block([[c2, s2], [-s2, c2]])
    f2i = np.block([[c2, -s2], [s2, c2]])
    ang4 = 2.0 * np.pi * np.arange(h)[:, None] * np.arange(h)[None, :] / n1
    wgt = np.where(np.arange(h)[None, :] == 0, 1.0, 2.0)
    g4 = np.concatenate([wgt * np.cos(ang4), -wgt * np.sin(ang4)], axis=1)
    sgn = np.broadcast_to(((-1.0) ** np.arange(h))[:, None], (h, 2 * LANES))
    f32 = lambda a: jnp.asarray(a, F32)
    return dict(f1=f32(f1), f1h=f32(f1[:, :n1 // 2]), tc=f32(tc), ts=f32(ts), f2f=f32(f2f), f2i=f32(f2i),
                g4=f32(g4), sgn=f32(sgn))


def _swap(a):
    return jnp.concatenate([a[..., LANES:], a[..., :LANES]], axis=-1)


def _stage1_pair(f1, z0, z1, tc, ts):
    zp = jnp.concatenate([z0, z1], axis=1).astype(BF16)
    aa = jnp.dot(f1, zp, preferred_element_type=F32)
    nr = aa.shape[0] // 2
    outs = []
    for q in range(2):
        a = jnp.concatenate([aa[:nr, q * LANES:(q + 1) * LANES], aa[nr:, q * LANES:(q + 1) * LANES]], axis=1)
        a = a.astype(BF16)
        outs.append(a * tc + _swap(a) * ts)
    return outs


def _filt_gen_kernel(w1t_ref, b1_ref, w2t_ref, b2_ref, fr_ref, w3t_ref, dl_ref, k_ref, ssq_ref, *, l, tt):
    i = pl.program_id(0)
    inv_lm1 = 1.0 / (l - 1)
    n = i * tt + lax.broadcasted_iota(jnp.int32, (1, tt), 1)
    j = jnp.where(n < l, n, 2 * l - n).astype(F32)
    t = j * inv_lm1
    w = (2.0 * math.pi) * j / l
    bands = 1e-4 + lax.broadcasted_iota(jnp.int32, (HY_BANDS, 1), 0).astype(F32) * ((HY_BANDS - 1 - 1e-4) / (HY_BANDS - 1))
    fw = bands * w
    fr = fr_ref[...]
    z1 = (w1t_ref[:, 0:1] * t
          + jnp.dot(w1t_ref[:, 1:1 + HY_BANDS], jnp.cos(fw), precision=HI, preferred_element_type=F32)
          - jnp.dot(w1t_ref[:, 1 + HY_BANDS:], jnp.sin(fw), precision=HI, preferred_element_type=F32)
          + b1_ref[...])
    h1 = jnp.sin(fr * z1)
    h2 = jnp.sin(fr * (jnp.dot(w2t_ref[...], h1, precision=HI, preferred_element_type=F32) + b2_ref[...]))
    ht = jnp.dot(w3t_ref[...], h2.astype(BF16), preferred_element_type=F32)
    dec = jnp.exp(-t * dl_ref[...])
    dec = jnp.where(n == l, 0.0, dec)
    k = ht * jnp.concatenate([dec] * HY_ORDER, axis=0)
    k_ref[...] = k.astype(k_ref.dtype)

    @pl.when(i == 0)
    def _():
        ssq_ref[...] = jnp.zeros_like(ssq_ref)

    k2 = k * k
    part = k2[:, 0:LANES]
    for q in range(1, tt // LANES):
        part = part + k2[:, q * LANES:(q + 1) * LANES]
    ssq_ref[...] += part


def _filt_gen(hw1, hb1, hw2, hb2, hw3, hfreq, l, *, tt=1024):
    n = 2 * l
    nt = n // tt
    rows = HY_ORDER * D_B
    w3t = hw3.T.reshape(2, rows, HY_HIDDEN).astype(BF16)
    max_decay = math.log(HY_TARGET) / HY_FAST_PCT
    min_decay = math.log(HY_TARGET) / HY_SLOW_PCT
    deltas = jnp.abs(jnp.linspace(min_decay, max_decay, D_B, dtype=F32)).reshape(D_B, 1)
    col = lambda a: a.reshape(HY_HIDDEN, 1)
    kern = functools.partial(_filt_gen_kernel, l=l, tt=tt)
    return pl.pallas_call(
        kern,
        grid=(nt,),
        in_specs=[_const_spec((HY_HIDDEN, 2 * HY_BANDS + 1)), _const_spec((HY_HIDDEN, 1)),
                  _const_spec((HY_HIDDEN, HY_HIDDEN)), _const_spec((HY_HIDDEN, 1)), _const_spec((HY_HIDDEN, 1)),
                  pl.BlockSpec((None, rows, HY_HIDDEN), lambda i: (jnp.where(i * tt >= l, 1, 0), 0, 0)),
                  _const_spec((D_B, 1))],
        out_specs=[pl.BlockSpec((rows, tt), lambda i: (0, i)),
                   pl.BlockSpec((rows, LANES), lambda i: (0, 0))],
        out_shape=[jax.ShapeDtypeStruct((rows, n), BF16), jax.ShapeDtypeStruct((rows, LANES), F32)],
        compiler_params=_params(("arbitrary",)),
        name="hyena_filter_gen",
    )(hw1.T, col(hb1), hw2.T, col(hb2), col(hfreq), w3t, deltas)


def _filt_fft_kernel(k_ref, ssq_ref, f1_ref, tc_ref, ts_ref, f2_ref, o_ref, *, ct, n_total):
    tc, ts = tc_ref[...].astype(BF16), ts_ref[...].astype(BF16)
    f1, f2 = f1_ref[...].astype(BF16), f2_ref[...].astype(BF16)
    nr = tc.shape[0]
    b = []
    for p in range(ct // 2):
        b0, b1 = _stage1_pair(f1, k_ref[2 * p], k_ref[2 * p + 1], tc, ts)
        b.append(jnp.concatenate([b0, b1], axis=0))
    x = [jnp.dot(bp, f2, preferred_element_type=F32) for bp in b]
    for p in range(ct // 2):
        for q in range(2):
            c = 2 * p + q
            tot = jnp.sum(ssq_ref[c], axis=-1, keepdims=True)
            o_ref[c] = (x[p][q * nr:(q + 1) * nr] * (lax.rsqrt(tot + EPS) * (1.0 / n_total))).astype(o_ref.dtype)


def _filt_fft(k, ssq, tabs, l, *, ct=8):
    rows = k.shape[0]
    n = 2 * l
    n1 = n // LANES
    nr = tabs["tc"].shape[0]
    kern = functools.partial(_filt_fft_kernel, ct=ct, n_total=n)
    return pl.pallas_call(
        kern,
        grid=(rows // ct,),
        in_specs=[pl.BlockSpec((ct, n1, LANES), lambda c: (c, 0, 0)),
                  pl.BlockSpec((ct, 1, LANES), lambda c: (c, 0, 0)),
                  _const_spec(tabs["f1"].shape), _const_spec(tabs["tc"].shape), _const_spec(tabs["ts"].shape),
                  _const_spec(tabs["f2f"].shape)],
        out_specs=pl.BlockSpec((ct, nr, 2 * LANES), lambda c: (c, 0, 0)),
        out_shape=jax.ShapeDtypeStruct((rows, nr, 2 * LANES), BF16),
        compiler_params=_params(("arbitrary",)),
        name="hyena_filter_fft",
    )(k.reshape(rows, n1, LANES), ssq.reshape(rows, 1, LANES), tabs["f1"], tabs["tc"], tabs["ts"], tabs["f2f"])


def _shift_prev(a, row, lane):
    nv = a.shape[0] // SUBLANES
    r3 = pltpu.roll(a, 1, axis=1).reshape(nv, SUBLANES, LANES)
    rot = pltpu.roll(r3, 1, axis=1)
    rot_prev = jnp.concatenate([jnp.zeros((1, SUBLANES, LANES), a.dtype), rot[:-1]], axis=0)
    out = jnp.where((lane == 0) & (row == 0), rot_prev, jnp.where(lane == 0, rot, r3))
    return out.reshape(a.shape)


def _shift_next(a, row, lane):
    nv = a.shape[0] // SUBLANES
    r3 = pltpu.roll(a, LANES - 1, axis=1).reshape(nv, SUBLANES, LANES)
    rot = pltpu.roll(r3, SUBLANES - 1, axis=1)
    rot_next = jnp.concatenate([rot[1:], jnp.zeros((1, SUBLANES, LANES), a.dtype)], axis=0)
    out = jnp.where((lane == LANES - 1) & (row == SUBLANES - 1), rot_next, jnp.where(lane == LANES - 1, rot, r3))
    return out.reshape(a.shape)


def _hyena_kernel(sw_ref, sb_ref, skip_ref, bp_ref, kf_ref, f1_ref, tc_ref, ts_ref, f2f_ref, f2i_ref, g4_ref,
                  sgn_ref, o_ref, *, ct):
    ci = pl.program_id(1)
    nh = bp_ref.shape[2]
    nr = tc_ref.shape[0]
    npair = ct // 2
    row = lax.broadcasted_iota(jnp.int32, (nh // SUBLANES, SUBLANES, LANES), 1)
    lane = lax.broadcasted_iota(jnp.int32, (nh // SUBLANES, SUBLANES, LANES), 2)
    tc, ts = tc_ref[...].astype(BF16), ts_ref[...].astype(BF16)
    f1, f2f, f2i, g4 = (r[...].astype(BF16) for r in (f1_ref, f2f_ref, f2i_ref, g4_ref))

    def short_conv(part, c):
        ch = part * D_B + ci * ct + c
        a = bp_ref[part, c]
        return (sw_ref[0, ch] * _shift_prev(a, row, lane) + sw_ref[1, ch] * a
                + sw_ref[2, ch] * _shift_next(a, row, lane) + sb_ref[ch])

    def stage1(z0, z1):
        b0, b1 = _stage1_pair(f1, z0, z1, tc, ts)
        return jnp.concatenate([b0, b1], axis=0)

    def stage2(b):
        return jnp.dot(b, f2f, preferred_element_type=F32)

    def filt(x, order, p):
        outs = []
        for q in range(2):
            kf = kf_ref[order, 2 * p + q]
            kr, ki = kf[:, :LANES], kf[:, LANES:]
            xq = x[q * nr:(q + 1) * nr].astype(BF16)
            outs.append(xq * jnp.concatenate([kr, kr], axis=-1) + _swap(xq) * jnp.concatenate([-ki, ki], axis=-1))
        return jnp.concatenate(outs, axis=0)

    def stage3(y):
        return jnp.dot(y, f2i, preferred_element_type=F32)

    def twiddle_inv(cm):
        cm = cm.astype(BF16)
        d0 = cm[:nr] * tc - _swap(cm[:nr]) * ts
        d1 = cm[nr:] * tc - _swap(cm[nr:]) * ts
        rhs = jnp.concatenate(
            [jnp.concatenate([d0[:nh, :LANES], d1[:nh, :LANES]], axis=1),
             jnp.concatenate([d0[:nh, LANES:], d1[:nh, LANES:]], axis=1)], axis=0)
        mid = jnp.concatenate([d0[nh:nh + 1, :LANES], d1[nh:nh + 1, :LANES]], axis=1).astype(F32)
        return rhs, mid

    def stage4(d):
        rhs, mid = d
        return jnp.dot(g4, rhs, preferred_element_type=F32) + sgn_ref[...] * mid

    def long_conv(zs, order):
        b = [stage1(*zs[p]) for p in range(npair)]
        x = [stage2(b[p]) for p in range(npair)]
        y = [filt(x[p], order, p) for p in range(npair)]
        c = [stage3(y[p]) for p in range(npair)]
        d = [twiddle_inv(c[p]) for p in range(npair)]
        o = [stage4(d[p]) for p in range(npair)]
        return [(o[p][:, :LANES], o[p][:, LANES:]) for p in range(npair)]

    v = [(short_conv(0, 2 * p), short_conv(0, 2 * p + 1)) for p in range(npair)]
    y1 = long_conv(v, 0)
    z1 = []
    for p in range(npair):
        pr = []
        for q in range(2):
            c = 2 * p + q
            pr.append(short_conv(1, c) * (y1[p][q] + v[p][q] * skip_ref[0, ci * ct + c]))
        z1.append(tuple(pr))
    y2 = long_conv(z1, 1)
    for p in range(npair):
        for q in range(2):
            c = 2 * p + q
            o_ref[c] = (short_conv(2, c) * (y2[p][q] + z1[p][q] * skip_ref[1, ci * ct + c])).astype(o_ref.dtype)


def _hyena(bpt, kf, short_w, short_b, skip, tabs, *, ct=8):
    b, _, l = bpt.shape
    nh = l // LANES
    nr = tabs["tc"].shape[0]
    smem = pl.BlockSpec(memory_space=pltpu.SMEM)
    kern = functools.partial(_hyena_kernel, ct=ct)
    out = pl.pallas_call(
        kern,
        grid=(b, D_B // ct),
        in_specs=[smem, smem, smem,
                  pl.BlockSpec((None, 3, ct, nh, LANES), lambda bi, c: (bi, 0, c, 0, 0)),
                  pl.BlockSpec((HY_ORDER, ct, nr, 2 * LANES), lambda bi, c: (0, c, 0, 0)),
                  _const_spec(tabs["f1h"].shape), _const_spec(tabs["tc"].shape), _const_spec(tabs["ts"].shape),
                  _const_spec(tabs["f2f"].shape), _const_spec(tabs["f2i"].shape), _const_spec(tabs["g4"].shape),
                  _const_spec(tabs["sgn"].shape)],
        out_specs=pl.BlockSpec((None, ct, nh, LANES), lambda bi, c: (bi, c, 0, 0)),
        out_shape=jax.ShapeDtypeStruct((b, D_B, nh, LANES), BF16),
        compiler_params=_params(("arbitrary", "arbitrary")),
        name="hyena_conv",
    )(short_w, short_b, skip, bpt.reshape(b, 3, D_B, nh, LANES), kf.reshape(HY_ORDER, D_B, nr, 2 * LANES),
      tabs["f1h"], tabs["tc"], tabs["ts"], tabs["f2f"], tabs["f2i"], tabs["g4"], tabs["sgn"])
    return out.reshape(b, D_B, l)


def _alibi_slopes():
    return [2.0 ** (-8.0 * (i + 1) / N_HEADS) for i in range(N_HEADS)]


def _attn_kernel(sink_ref, b0_ref, b1_ref, b2_ref, x_ref, q_ref, kp_ref, kc_ref, kn_ref, vp_ref, vc_ref, vn_ref,
                 mod_ref, wo_ref, o_ref, kall_ref, vall_ref, oh_ref, *, tq, ahead):
    i = pl.program_id(1)
    nk = N_KV * HEAD_DIM
    kall_ref[0:WINDOW, :] = kp_ref[...]
    kall_ref[WINDOW:WINDOW + tq, :] = kc_ref[...]
    kall_ref[WINDOW + tq:, :] = kn_ref[...]
    vall_ref[0:WINDOW, :] = vp_ref[...]
    vall_ref[WINDOW:WINDOW + tq, :] = vc_ref[...]
    vall_ref[WINDOW + tq:, :] = vn_ref[...]

    span = 3 * WINDOW
    lane = lax.broadcasted_iota(jnp.int32, (WINDOW, LANES), 1)
    low = lane < HEAD_DIM

    def sub_block(sb, carry):
        r0 = pl.multiple_of(sb * WINDOW, WINDOW)
        first = (i == 0) & (sb == 0)
        lastb = (i == pl.num_programs(1) - 1) & (sb == tq // WINDOW - 1)

        def cols_for(head):
            kvh = head // GROUP
            tile = kvh // 2
            plain = slice(tile * LANES, (tile + 1) * LANES)
            swapped = slice(nk + tile * LANES, nk + (tile + 1) * LANES)
            return plain if kvh % 2 == head % 2 else swapped

        def scores(head):
            pair = head // 2
            q2 = q_ref[pl.ds(r0, WINDOW), pair * LANES:(pair + 1) * LANES]
            qz = jnp.zeros_like(q2)
            qm = jnp.where(low, q2, qz) if head % 2 == 0 else jnp.where(low, qz, q2)
            kw = kall_ref[pl.ds(r0, span), cols_for(head)]
            s = lax.dot_general(qm, kw, (((1,), (1,)), ((), ())), preferred_element_type=F32)
            bias = jnp.concatenate([b0_ref[jnp.where(first, N_HEADS, head)], b1_ref[head],
                                    b2_ref[jnp.where(lastb, N_HEADS, head)]], axis=1)
            return s + bias

        pend = [scores(h) for h in range(ahead)]
        outs = []
        for head in range(N_HEADS):
            s = pend.pop(0)
            if head + ahead < N_HEADS:
                pend.append(scores(head + ahead))
            vw = vall_ref[pl.ds(r0, span), cols_for(head)]
            snk = sink_ref[head] * LOG2E
            m = jnp.maximum(jnp.max(s, axis=-1, keepdims=True), snk)
            p = jnp.exp2(s - m)
            den = jnp.sum(p, axis=-1, keepdims=True) + jnp.exp2(snk - m)
            outs.append(jnp.dot(p.astype(BF16), vw, preferred_element_type=F32) / den)
            if head % 2 == 1:
                pair = head // 2
                oh_ref[pl.ds(r0, WINDOW), pair * LANES:(pair + 1) * LANES] = (
                    jnp.where(low, outs[0], outs[1]).astype(BF16))
                outs = []
        return carry

    lax.fori_loop(0, tq // WINDOW, sub_block, 0)
    y = jnp.dot(oh_ref[...], wo_ref[...], preferred_element_type=F32)
    o_ref[...] = x_ref[...] + mod_ref[2:3, :] * y


def _attn(x, q, kk, vv, mod3, sink, w_out, *, tq=512, ahead=5):
    b, l, d = x.shape
    nq, nk2 = q.shape[-1], kk.shape[-1]
    r = tq // WINDOW
    nblk = l // WINDOW
    prev = lambda bi, i: (bi, jnp.maximum(i * r - 1, 0), 0)
    cur = lambda bi, i: (bi, i, 0)
    nxt = lambda bi, i: (bi, jnp.minimum((i + 1) * r, nblk - 1), 0)
    dist = np.abs(np.arange(3 * WINDOW)[None, :] - WINDOW - np.arange(WINDOW)[:, None])
    bias = -np.asarray(_alibi_slopes())[:, None, None] * dist[None] * LOG2E
    bias = np.where(dist[None] <= WINDOW, bias, -np.inf)
    gone = np.full((1, WINDOW, WINDOW), -np.inf)
    b0 = jnp.asarray(np.concatenate([bias[:, :, :WINDOW], gone]), F32)
    b1 = jnp.asarray(bias[:, :, WINDOW:2 * WINDOW], F32)
    b2 = jnp.asarray(np.concatenate([bias[:, :, 2 * WINDOW:], gone]), F32)
    kern = functools.partial(_attn_kernel, tq=tq, ahead=ahead)
    return pl.pallas_call(
        kern,
        grid=(b, l // tq),
        in_specs=[pl.BlockSpec(memory_space=pltpu.SMEM),
                  _const_spec(b0.shape), _const_spec(b1.shape), _const_spec(b2.shape),
                  pl.BlockSpec((None, tq, d), cur),
                  pl.BlockSpec((None, tq, nq), cur),
                  pl.BlockSpec((None, WINDOW, nk2), prev), pl.BlockSpec((None, tq, nk2), cur),
                  pl.BlockSpec((None, WINDOW, nk2), nxt),
                  pl.BlockSpec((None, WINDOW, nk2), prev), pl.BlockSpec((None, tq, nk2), cur),
                  pl.BlockSpec((None, WINDOW, nk2), nxt),
                  pl.BlockSpec((None, 3, d), lambda bi, i: (bi, 0, 0)),
                  _const_spec(w_out.shape)],
        out_specs=pl.BlockSpec((None, tq, d), cur),
        out_shape=jax.ShapeDtypeStruct(x.shape, F32),
        scratch_shapes=[pltpu.VMEM((tq + 2 * WINDOW, nk2), BF16), pltpu.VMEM((tq + 2 * WINDOW, nk2), BF16),
                        pltpu.VMEM((tq, nq), BF16)],
        compiler_params=_params(("arbitrary", "arbitrary")),
        name="attn",
    )(sink, b0, b1, b2, x, q, kk, kk, kk, vv, vv, vv, mod3, w_out)


def _trunk(x, mods, w, filt, tabs):
    m = [mods[0][:, 3 * s:3 * s + 3] for s in range(N_SUB)]
    x, a_glu, bpt = _ffn_fused(x, m[0], w["norm_g"][0, 0], w["ffn_in"][0][0], w["ffn_out"][0][0],
                               post=("proj", m[1], w["norm_g"][0, 1], w["ab_wa"], w["ab_wbt"]))
    zt = _hyena(bpt, filt, w["hy_short_w"], w["hy_short_b"], w["hy_skip"], tabs)
    x = _ffn_fused(x, m[2], w["norm_g"][0, 2], w["ffn_in"][0][1], w["ffn_out"][0][1],
                   pre=("mix", a_glu, zt, m[1][:, 2:3], w["ab_woa"], w["ab_woz"], w["conv_w"], w["conv_b"],
                        w["conv_ln_g"], w["conv_ln_b"]), halves=1)[0]

    m = [mods[1][:, 3 * s:3 * s + 3] for s in range(N_SUB)]
    x, q, kk, vv = _ffn_fused(x, m[0], w["norm_g"][1, 0], w["ffn_in"][1][0], w["ffn_out"][1][0],
                              post=("qkv", m[1], w["norm_g"][1, 1], w["attn_w_qkv"]))
    x = _attn(x, q, kk, vv, m[1], w["attn_sink"], w["attn_w_out"])
    x = _ffn(x, m[2], w["norm_g"][1, 2], w["ffn_in"][1][1], w["ffn_out"][1][1], w["final_g"], final=True)
    return x


def kernel(x_prompt, x_sample, c_prompt, c_sample, ada_w, ada_b, norm_g, ffn_w_in, ffn_w_out, final_g, ab_w_in,
           conv_w, conv_b, conv_ln_g, conv_ln_b, hy_short_w, hy_short_b, hy_w1, hy_b1, hy_w2, hy_b2, hy_w3,
           hy_freq, hy_skip, ab_w_out, attn_w_qkv, attn_sink, attn_w_out):
    l = x_prompt.shape[1]
    assert x_sample.shape[1] == l and l % 512 == 0
    bp, bs = x_prompt.shape[0], x_sample.shape[0]
    c_pad = jnp.zeros((8, D_MODEL), F32).at[:bp].set(c_prompt).at[bp:bp + bs].set(c_sample)
    mod = _ada_mod(c_pad, ada_w, ada_b)
    depth = mod.shape[0]
    mod = mod.reshape(depth, 8, 3 * N_SUB, D_MODEL)

    w_in16, w_out16 = _to_bf16(ffn_w_in), _to_bf16(ffn_w_out)
    wa, wbt, woa, woz, wqkv, wo = _mixer_weights(ab_w_in, ab_w_out, attn_w_qkv, attn_w_out)
    w = dict(
        norm_g=norm_g, final_g=final_g,
        ffn_in=[[(w_in16, (i, s)) for s in range(2)] for i in range(depth)],
        ffn_out=[[(w_out16, (i, s)) for s in range(2)] for i in range(depth)],
        ab_wa=wa, ab_wbt=wbt,
        conv_w=conv_w[0], conv_b=conv_b[0], conv_ln_g=conv_ln_g[0], conv_ln_b=conv_ln_b[0],
        hy_short_w=hy_short_w[0], hy_short_b=hy_short_b[0], hy_skip=hy_skip[0],
        ab_woa=woa, ab_woz=woz, attn_w_qkv=wqkv, attn_sink=attn_sink[0], attn_w_out=wo,
    )
    tabs = _fft_tables(l)
    kt, ssq = _filt_gen(hy_w1[0], hy_b1[0], hy_w2[0], hy_b2[0], hy_w3[0], hy_freq[0], l)
    filt = _filt_fft(kt, ssq, tabs, l)

    y_prompt = _trunk(x_prompt, mod[:, :bp], w, filt, tabs)
    y_sample = _trunk(x_sample, mod[:, bp:bp + bs], w, filt, tabs)
    return (y_prompt, y_sample)
```
